```python
import jax, jax.numpy as jnp
from jax import lax
import numpy as np

D_MODEL = 4096
BATCH = 2
SEQ = 8192
DEPTH = 2

MEM_LEN = 256
NORM_EPS = 1e-6
ROPE_THETA = 10000.0
NEG_INF = -1e30
N_BRANCHES = 4
BRANCH_W = D_MODEL // 4

SWA_HD = 128
SWA_HEADS = BRANCH_W // SWA_HD
SWA_PATTERNS = ((128, 1), (512, 4), (2048, 16))
SWA_GROUPS = len(SWA_PATTERNS)
LOCAL_BLOCK = 128

CONV_CH = BRANCH_W
CONV_K = 31

MLA_HEADS = BRANCH_W // 128
MLA_NOPE = 128
MLA_ROPE = 64
MLA_V = 128
MLA_Q_LORA = D_MODEL // 4
MLA_KV_LORA = 512
Q_BLOCK = 128

LRU_W = BRANCH_W
LRU_BLOCKS = 8
LRU_BW = LRU_W // LRU_BLOCKS
LRU_CONV_K = 4
LRU_C = 8.0

X_HEADS = 4
X_HD = 256
X_W = X_HEADS * X_HD

D_FF = 256 * (-(-8 * D_MODEL // (3 * 256)))

IN_WIDTHS = (SWA_GROUPS * SWA_HEADS * SWA_HD,
             SWA_HEADS * SWA_HD,
             SWA_HEADS * SWA_HD,
             2 * CONV_CH,
             MLA_Q_LORA,
             MLA_KV_LORA,
             MLA_ROPE,
             LRU_W,
             LRU_W)
N_IN = sum(IN_WIDTHS)

kernel_name = 'hybrid_gated_parallel_mixer_decoder'


def rms_norm(x, g):
    xf = x.astype(jnp.float32)
    y = xf * lax.rsqrt(jnp.mean(xf * xf, axis=-1, keepdims=True) + NORM_EPS)
    return (y * g.astype(jnp.float32)).astype(x.dtype)


def layer_norm(x, g, b):
    xf = x.astype(jnp.float32)
    mu = jnp.mean(xf, axis=-1, keepdims=True)
    var = jnp.mean(jnp.square(xf - mu), axis=-1, keepdims=True)
    y = (xf - mu) * lax.rsqrt(var + NORM_EPS)
    return (y * g.astype(jnp.float32) + b.astype(jnp.float32)).astype(x.dtype)


def rotary_tables(seq_len, dim):
    inv_freq = ROPE_THETA ** (-jnp.arange(0, dim, 2, dtype=jnp.float32) / dim)
    ang = jnp.arange(seq_len, dtype=jnp.float32)[:, None] * inv_freq[None, :]
    return jnp.cos(ang), jnp.sin(ang)


def apply_rotary(x, cos, sin):
    xf = x.astype(jnp.float32)
    x1, x2 = jnp.split(xf, 2, axis=-1)
    c = cos[None, :, None, :]
    s = sin[None, :, None, :]
    return jnp.concatenate([x1 * c - x2 * s, x2 * c + x1 * s], axis=-1).astype(x.dtype)


def causal_depthwise_conv(x, w, b):
    k = w.shape[0]
    y = lax.conv_general_dilated(x, w[:, None, :], window_strides=(1,), padding=[(k - 1, 0)],
                                 dimension_numbers=('NWC', 'WIO', 'NWC'),
                                 feature_group_count=x.shape[-1])
    return y + b


def dilated_window_attention(q, k, v, dilation, n_back):
    B, S, H, hd = q.shape
    span = dilation * LOCAL_BLOCK
    s_pad = -(-S // span) * span
    L = s_pad // dilation
    nb = L // LOCAL_BLOCK

    def to_blocks(t):
        t = jnp.pad(t, ((0, 0), (0, s_pad - S), (0, 0), (0, 0)))
        t = t.reshape(B, L, dilation, H, hd).transpose(0, 2, 1, 3, 4)
        return t.reshape(B, dilation, nb, LOCAL_BLOCK, H, hd)

    def with_prev(t):
        prev = jnp.pad(t[:, :, :-1], ((0, 0), (0, 0), (1, 0), (0, 0), (0, 0), (0, 0)))
        return jnp.concatenate([prev, t], axis=3)

    qb = to_blocks(q)
    kb = with_prev(to_blocks(k))
    vb = with_prev(to_blocks(v))
    scores = jnp.einsum('brnqhd,brnkhd->brnhqk', qb, kb,
                        preferred_element_type=jnp.float32) * (hd ** -0.5)
    qi = jnp.arange(LOCAL_BLOCK)[:, None] + LOCAL_BLOCK
    ki = jnp.arange(2 * LOCAL_BLOCK)[None, :]
    dist = qi - ki
    in_band = (dist >= 0) & (dist <= n_back)
    has_prev = (jnp.arange(nb)[:, None, None] > 0) | (ki[None] >= LOCAL_BLOCK)
    mask = in_band[None] & has_prev
    scores = jnp.where(mask[None, None, :, None], scores, NEG_INF)
    lse = jax.nn.logsumexp(scores, axis=-1)
    p = jnp.exp(scores - lse[..., None]).astype(v.dtype)
    o = jnp.einsum('brnhqk,brnkhd->brnqhd', p, vb)
    o = o.reshape(B, dilation, L, H, hd).transpose(0, 2, 1, 3, 4).reshape(B, s_pad, H, hd)[:, :S]
    lse = lse.transpose(0, 1, 2, 4, 3).reshape(B, dilation, L, H)
    lse = lse.transpose(0, 2, 1, 3).reshape(B, s_pad, H)[:, :S]
    return o, lse


def causal_attention_blocked(q, k, v, scale):
    B, S, H, dk = q.shape
    dv = v.shape[-1]
    nb = S // Q_BLOCK
    qb = q.reshape(B, nb, Q_BLOCK, H, dk).transpose(1, 0, 2, 3, 4)
    k_pos = jnp.arange(S)

    def one_block(args):
        q_blk, i = args
        s = jnp.einsum('bqhd,bkhd->bhqk', q_blk, k, preferred_element_type=jnp.float32) * scale
        q_pos = i * Q_BLOCK + jnp.arange(Q_BLOCK)
        s = jnp.where(k_pos[None, :] <= q_pos[:, None], s, NEG_INF)
        p = jax.nn.softmax(s, axis=-1).astype(v.dtype)
        return jnp.einsum('bhqk,bkhd->bqhd', p, v)

    out = lax.map(one_block, (qb, jnp.arange(nb)))
    return out.transpose(1, 0, 2, 3, 4).reshape(B, S, H, dv)


def rg_lru(u, w_a, b_a, w_x, b_x, lam):
    B, S, W = u.shape
    uf = u.astype(jnp.float32)
    ub = uf.reshape(B, S, LRU_BLOCKS, LRU_BW)
    r = jax.nn.sigmoid(jnp.einsum('bsnc,ncd->bsnd', ub, w_a.astype(jnp.float32)).reshape(B, S, W)
                       + b_a.astype(jnp.float32))
    i = jax.nn.sigmoid(jnp.einsum('bsnc,ncd->bsnd', ub, w_x.astype(jnp.float32)).reshape(B, S, W)
                       + b_x.astype(jnp.float32))
    log_a = -LRU_C * r * jax.nn.softplus(-lam.astype(jnp.float32))
    a = jnp.exp(log_a)
    b = jnp.sqrt(-jnp.expm1(2.0 * log_a)) * (i * uf)

    def combine(c1, c2):
        a1, b1 = c1
        a2, b2 = c2
        return a1 * a2, a2 * b1 + b2

    _, h = lax.associative_scan(combine, (a, b), axis=1)
    return h.astype(u.dtype)


def hybrid_mixer(xn, rope_a, rope_c, w_in, w_mla_q_up, g_mla_q, w_mla_kv_up, g_mla_kv,
                 w_conf_dw, b_conf_dw, g_conf_ln, b_conf_ln, w_lru_conv, b_lru_conv,
                 w_lru_a, b_lru_a, w_lru_x, b_lru_x, lru_lambda, w_branch_out, w_gate,
                 b_gate, w_o):
    B, S, _ = xn.shape
    cuts = np.cumsum(IN_WIDTHS)[:-1].tolist()
    qa, ka, va, conf_in, c_q, c_kv, k_pe, lru_in, lru_gate = jnp.split(xn @ w_in, cuts, axis=-1)

    cos_a, sin_a = rope_a
    qa = apply_rotary(qa.reshape(B, S, SWA_GROUPS * SWA_HEADS, SWA_HD), cos_a, sin_a)
    qa = qa.reshape(B, S, SWA_GROUPS, SWA_HEADS, SWA_HD)
    ka = apply_rotary(ka.reshape(B, S, SWA_HEADS, SWA_HD), cos_a, sin_a)
    va = va.reshape(B, S, SWA_HEADS, SWA_HD)
    outs, lses = [], []
    for g, (window, dilation) in enumerate(SWA_PATTERNS):
        o, l = dilated_window_attention(qa[:, :, g], ka, va, dilation, window // dilation)
        outs.append(o)
        lses.append(l)
    wts = jax.nn.softmax(jnp.stack(lses), axis=0)
    y_a = jnp.einsum('gbsh,gbshd->bshd', wts, jnp.stack(outs).astype(jnp.float32))
    y_a = y_a.reshape(B, S, BRANCH_W).astype(xn.dtype)

    val, gate = jnp.split(conf_in, 2, axis=-1)
    u = causal_depthwise_conv(val * jax.nn.sigmoid(gate), w_conf_dw, b_conf_dw)
    y_b = jax.nn.silu(layer_norm(u, g_conf_ln, b_conf_ln))

    cos_c, sin_c = rope_c
    q = (rms_norm(c_q, g_mla_q) @ w_mla_q_up).reshape(B, S, MLA_HEADS, MLA_NOPE + MLA_ROPE)
    q_nope, q_pe = jnp.split(q, [MLA_NOPE], axis=-1)
    kv = (rms_norm(c_kv, g_mla_kv) @ w_mla_kv_up).reshape(B, S, MLA_HEADS, MLA_NOPE + MLA_V)
    k_nope, v_c = jnp.split(kv, [MLA_NOPE], axis=-1)
    k_pe = apply_rotary(k_pe[:, :, None, :], cos_c, sin_c)
    q_c = jnp.concatenate([q_nope, apply_rotary(q_pe, cos_c, sin_c)], axis=-1)
    k_c = jnp.concatenate([k_nope, jnp.broadcast_to(k_pe, (B, S, MLA_HEADS, MLA_ROPE))], axis=-1)
    y_c = causal_attention_blocked(q_c, k_c, v_c, (MLA_NOPE + MLA_ROPE) ** -0.5)
    y_c = y_c.reshape(B, S, BRANCH_W)

    u = causal_depthwise_conv(lru_in, w_lru_conv, b_lru_conv)
    y_d = rg_lru(u, w_lru_a, b_lru_a, w_lru_x, b_lru_x, lru_lambda) * jax.nn.gelu(lru_gate)

    merged = jnp.zeros_like(xn)
    for i, y in enumerate((y_a, y_b, y_c, y_d)):
        merged = merged + jax.nn.sigmoid(xn @ w_gate[i] + b_gate[i]) * (y @ w_branch_out[i])
    return merged @ w_o


def memory_cross_attention(hn, mem_n, w_xq, w_xk, w_xv, w_xo):
    B, S, _ = hn.shape
    M = mem_n.shape[1]
    q = (hn @ w_xq).reshape(B, S, X_HEADS, X_HD)
    k = (mem_n @ w_xk).reshape(B, M, X_HEADS, X_HD)
    v = (mem_n @ w_xv).reshape(B, M, X_HEADS, X_HD)
    s = jnp.einsum('bqhd,bkhd->bhqk', q, k, preferred_element_type=jnp.float32) * (X_HD ** -0.5)
    p = jax.nn.softmax(s, axis=-1).astype(v.dtype)
    o = jnp.einsum('bhqk,bkhd->bqhd', p, v).reshape(B, S, X_W)
    return o @ w_xo


def swiglu_ffn(hn, w_gate_ffn, w_up, w_down):
    return (jax.nn.silu(hn @ w_gate_ffn) * (hn @ w_up)) @ w_down


def setup_inputs(seed: int = 0) -> dict:
    key = jax.random.key(seed)
    ks = iter(jax.random.split(key, 48))
    L = DEPTH

    def normal(shape, scale):
        return jax.random.normal(next(ks), shape, jnp.float32) * scale

    def gain(shape):
        return 1.0 + 0.05 * jax.random.normal(next(ks), shape, jnp.float32)

    def bias(shape):
        return 0.02 * jax.random.normal(next(ks), shape, jnp.float32)

    u = jax.random.uniform(next(ks), (L, LRU_W), jnp.float32, 0.9, 0.999)
    a0 = u ** (1.0 / LRU_C)
    lru_lambda = jnp.log(a0) - jnp.log1p(-a0)

    return {
        'x': normal((BATCH, SEQ, D_MODEL), 1.0),
        'mem': normal((BATCH, MEM_LEN, D_MODEL), 1.0),
        'g_mix_pre': gain((L, D_MODEL)),
        'w_in': normal((L, D_MODEL, N_IN), D_MODEL ** -0.5),
        'w_mla_q_up': normal((L, MLA_Q_LORA, MLA_HEADS * (MLA_NOPE + MLA_ROPE)), MLA_Q_LORA ** -0.5),
        'g_mla_q': gain((L, MLA_Q_LORA)),
        'w_mla_kv_up': normal((L, MLA_KV_LORA, MLA_HEADS * (MLA_NOPE + MLA_V)), MLA_KV_LORA ** -0.5),
        'g_mla_kv': gain((L, MLA_KV_LORA)),
        'w_conf_dw': normal((L, CONV_K, CONV_CH), CONV_K ** -0.5),
        'b_conf_dw': bias((L, CONV_CH)),
        'g_conf_ln': gain((L, CONV_CH)),
        'b_conf_ln': bias((L, CONV_CH)),
        'w_lru_conv': normal((L, LRU_CONV_K, LRU_W), LRU_CONV_K ** -0.5),
        'b_lru_conv': bias((L, LRU_W)),
        'w_lru_a': normal((L, LRU_BLOCKS, LRU_BW, LRU_BW), LRU_BW ** -0.5),
        'b_lru_a': bias((L, LRU_W)),
        'w_lru_x': normal((L, LRU_BLOCKS, LRU_BW, LRU_BW), LRU_BW ** -0.5),
        'b_lru_x': bias((L, LRU_W)),
        'lru_lambda': lru_lambda,
        'w_branch_out': normal((L, N_BRANCHES, BRANCH_W, D_MODEL), BRANCH_W ** -0.5),
        'w_gate': normal((L, N_BRANCHES, D_MODEL, D_MODEL), D_MODEL ** -0.5),
        'b_gate': bias((L, N_BRANCHES, D_MODEL)),
        'w_o': normal((L, D_MODEL, D_MODEL), D_MODEL ** -0.5),
        'g_mix_post': gain((L, D_MODEL)),
        'g_x_pre': gain((L, D_MODEL)),
        'g_mem': gain((L, D_MODEL)),
        'w_xq': normal((L, D_MODEL, X_W), D_MODEL ** -0.5),
        'w_xk': normal((L, D_MODEL, X_W), D_MODEL ** -0.5),
        'w_xv': normal((L, D_MODEL, X_W), D_MODEL ** -0.5),
        'w_xo': normal((L, X_W, D_MODEL), X_W ** -0.5),
        'g_x_post': gain((L, D_MODEL)),
        'g_ffn_pre': gain((L, D_MODEL)),
        'w_ffn_gate': normal((L, D_MODEL, D_FF), D_MODEL ** -0.5),
        'w_ffn_up': normal((L, D_MODEL, D_FF), D_MODEL ** -0.5),
        'w_ffn_down': normal((L, D_FF, D_MODEL), D_FF ** -0.5),
        'g_ffn_post': gain((L, D_MODEL)),
    }


def reference(x, mem, g_mix_pre, w_in, w_mla_q_up, g_mla_q, w_mla_kv_up, g_mla_kv,
              w_conf_dw, b_conf_dw, g_conf_ln, b_conf_ln, w_lru_conv, b_lru_conv,
              w_lru_a, b_lru_a, w_lru_x, b_lru_x, lru_lambda, w_branch_out, w_gate, b_gate,
              w_o, g_mix_post, g_x_pre, g_mem, w_xq, w_xk, w_xv, w_xo, g_x_post,
              g_ffn_pre, w_ffn_gate, w_ffn_up, w_ffn_down, g_ffn_post):
    S = x.shape[1]
    rope_a = rotary_tables(S, SWA_HD)
    rope_c = rotary_tables(S, MLA_ROPE)
    h = x
    for l in range(DEPTH):
        hn = rms_norm(h, g_mix_pre[l])
        mix = hybrid_mixer(hn, rope_a, rope_c, w_in[l], w_mla_q_up[l], g_mla_q[l],
                           w_mla_kv_up[l], g_mla_kv[l], w_conf_dw[l], b_conf_dw[l],
                           g_conf_ln[l], b_conf_ln[l], w_lru_conv[l], b_lru_conv[l],
                           w_lru_a[l], b_lru_a[l], w_lru_x[l], b_lru_x[l], lru_lambda[l],
                           w_branch_out[l], w_gate[l], b_gate[l], w_o[l])
        h = h + rms_norm(mix, g_mix_post[l])

        hn = rms_norm(h, g_x_pre[l])
        mem_n = rms_norm(mem, g_mem[l])
        xa = memory_cross_attention(hn, mem_n, w_xq[l], w_xk[l], w_xv[l], w_xo[l])
        h = h + rms_norm(xa, g_x_post[l])

        hn = rms_norm(h, g_ffn_pre[l])
        h = h + rms_norm(swiglu_ffn(hn, w_ffn_gate[l], w_ffn_up[l], w_ffn_down[l]), g_ffn_post[l])
    return h
```

```python
import functools

import jax
import jax.numpy as jnp
import numpy as np
from jax import lax
from jax.experimental import pallas as pl
from jax.experimental.pallas import tpu as pltpu

NORM_EPS = 1e-6
ROPE_THETA = 10000.0
NEG_INF = -1e30

SWA_HD = 128
SWA_PATTERNS = ((128, 1), (512, 4), (2048, 16))
LOCAL_BLOCK = 128
CONV_K = 31
MLA_NOPE = 128
MLA_ROPE = 64
MLA_V = 128
MLA_QK = 256
LRU_BLOCKS = 8
LRU_CONV_K = 4
LRU_C = 8.0
X_HEADS = 4
X_HD = 256
LANES = 128
VMEM_LIMIT = 48 * 1024 * 1024

BF16 = jnp.bfloat16
F32 = jnp.float32


def _cparams(n_axes):
    return pltpu.CompilerParams(dimension_semantics=("arbitrary",) * n_axes,
                                vmem_limit_bytes=VMEM_LIMIT)


def _dot(a, b):
    return jnp.dot(a, b, preferred_element_type=F32)


def _dot_nt(a, b):
    return lax.dot_general(a, b, (((1,), (1,)), ((), ())), preferred_element_type=F32)


def _rms(x, g):
    return x * lax.rsqrt(jnp.mean(x * x, axis=-1, keepdims=True) + NORM_EPS) * g


def _sigmoid(x):
    return 1.0 / (1.0 + jnp.exp(-x))


def _silu(x):
    return x * _sigmoid(x)


def _gelu_tanh(x):
    return 0.5 * x * (1.0 + jnp.tanh(np.sqrt(2.0 / np.pi) * (x + 0.044715 * (x * x * x))))


def _rot_half(x, cos, sin):
    return x * cos + pltpu.roll(x, 64, 1) * sin


def _rms_cast_kernel(x_ref, g_ref, o_ref):
    o_ref[...] = _rms(x_ref[...], g_ref[...]).astype(o_ref.dtype)


def rms_cast(x, g, tm=256):
    m, d = x.shape
    return pl.pallas_call(
        _rms_cast_kernel,
        grid=(m // tm,),
        in_specs=[pl.BlockSpec((tm, d), lambda i: (i, 0)),
                  pl.BlockSpec((1, d), lambda i: (0, 0))],
        out_specs=pl.BlockSpec((tm, d), lambda i: (i, 0)),
        out_shape=jax.ShapeDtypeStruct((m, d), BF16),
        compiler_params=_cparams(1),
        name="rms_cast",
    )(x, g.reshape(1, d))


def _resid_norm_kernel(h_ref, y_ref, gpost_ref, gpre_ref, h_out_ref, hn_ref):
    h_new = h_ref[...] + _rms(y_ref[...], gpost_ref[...])
    h_out_ref[...] = h_new
    hn_ref[...] = _rms(h_new, gpre_ref[...]).astype(hn_ref.dtype)


def _resid_kernel(h_ref, y_ref, gpost_ref, h_out_ref):
    h_out_ref[...] = h_ref[...] + _rms(y_ref[...], gpost_ref[...])


def resid_norm(h, y, g_post, g_pre, tm=128):
    m, d = h.shape
    row = pl.BlockSpec((tm, d), lambda i: (i, 0))
    vec = pl.BlockSpec((1, d), lambda i: (0, 0))
    if g_pre is None:
        return pl.pallas_call(
            _resid_kernel, grid=(m // tm,),
            in_specs=[row, row, vec], out_specs=row,
            out_shape=jax.ShapeDtypeStruct((m, d), F32),
            compiler_params=_cparams(1), name="resid",
        )(h, y, g_post.reshape(1, d)), None
    return pl.pallas_call(
        _resid_norm_kernel, grid=(m // tm,),
        in_specs=[row, row, vec, vec], out_specs=[row, row],
        out_shape=[jax.ShapeDtypeStruct((m, d), F32), jax.ShapeDtypeStruct((m, d), BF16)],
        compiler_params=_cparams(1), name="resid_norm",
    )(h, y, g_post.reshape(1, d), g_pre.reshape(1, d))


def _mm_kernel(a_ref, w_ref, o_ref, *, act):
    acc = _dot(a_ref[...], w_ref[...])
    if act == "gelu":
        acc = _gelu_tanh(acc)
    o_ref[...] = acc.astype(o_ref.dtype)


def matmul(a, w, out_dtype, tm, tn, act=None, name="mm"):
    m, k = a.shape
    n = w.shape[1]
    return pl.pallas_call(
        functools.partial(_mm_kernel, act=act),
        grid=(m // tm, n // tn),
        in_specs=[pl.BlockSpec((tm, k), lambda i, j: (i, 0)),
                  pl.BlockSpec((k, tn), lambda i, j: (0, j))],
        out_specs=pl.BlockSpec((tm, tn), lambda i, j: (i, j)),
        out_shape=jax.ShapeDtypeStruct((m, n), out_dtype),
        compiler_params=_cparams(2), name=name,
    )(a, w)


def _dual_kernel(a_ref, w1_ref, w2_ref, o_ref, *, mode):
    a = a_ref[...]
    p1 = _dot(a, w1_ref[...])
    p2 = _dot(a, w2_ref[...])
    if mode == "glu":
        r = p1 * _sigmoid(p2)
    else:
        r = _silu(p1) * p2
    o_ref[...] = r.astype(o_ref.dtype)


def dual_matmul(a, w1, w2, mode, out_dtype, tm, tn, name):
    m, k = a.shape
    n = w1.shape[1]
    wspec = pl.BlockSpec((k, tn), lambda i, j: (0, j))
    return pl.pallas_call(
        functools.partial(_dual_kernel, mode=mode),
        grid=(m // tm, n // tn),
        in_specs=[pl.BlockSpec((tm, k), lambda i, j: (i, 0)), wspec, wspec],
        out_specs=pl.BlockSpec((tm, tn), lambda i, j: (i, j)),
        out_shape=jax.ShapeDtypeStruct((m, n), out_dtype),
        compiler_params=_cparams(2), name=name,
    )(a, w1, w2)


def _qkv_kernel(a_ref, w_ref, cos_ref, sin_ref, o_ref, *, tn, n_q, n_k, scale):
    j = pl.program_id(1)
    acc = _dot(a_ref[...], w_ref[...])

    def rotary(s):
        cos = cos_ref[...]
        sin = sin_ref[...]
        for c in range(tn // SWA_HD):
            sl = slice(c * SWA_HD, (c + 1) * SWA_HD)
            o_ref[:, sl] = (_rot_half(acc[:, sl], cos, sin) * s).astype(o_ref.dtype)

    @pl.when(j < n_q)
    def _():
        rotary(scale)

    @pl.when((j >= n_q) & (j < n_q + n_k))
    def _():
        rotary(1.0)

    @pl.when(j >= n_q + n_k)
    def _():
        o_ref[...] = acc.astype(o_ref.dtype)


def qkv_matmul(xn, w_qkv, cos2, sin2, seq, q_cols, k_cols, tm=1024, tn=512):
    m, k = xn.shape
    n = w_qkv.shape[1]
    nsb = seq // tm
    tab = pl.BlockSpec((tm, SWA_HD), lambda i, j: (i % nsb, 0))
    return pl.pallas_call(
        functools.partial(_qkv_kernel, tn=tn, n_q=q_cols // tn, n_k=k_cols // tn,
                          scale=SWA_HD ** -0.5),
        grid=(m // tm, n // tn),
        in_specs=[pl.BlockSpec((tm, k), lambda i, j: (i, 0)),
                  pl.BlockSpec((k, tn), lambda i, j: (0, j)), tab, tab],
        out_specs=pl.BlockSpec((tm, tn), lambda i, j: (i, j)),
        out_shape=jax.ShapeDtypeStruct((m, n), BF16),
        compiler_params=_cparams(2), name="qkv_rope",
    )(xn, w_qkv, cos2, sin2)


def _cq_kernel(a_ref, w_ref, g_ref, o_ref):
    o_ref[...] = _rms(_dot(a_ref[...], w_ref[...]), g_ref[...]).astype(o_ref.dtype)


def cq_matmul(xn, w, g, tm=512):
    m, k = xn.shape
    n = w.shape[1]
    return pl.pallas_call(
        _cq_kernel, grid=(m // tm,),
        in_specs=[pl.BlockSpec((tm, k), lambda i: (i, 0)),
                  pl.BlockSpec((k, n), lambda i: (0, 0)),
                  pl.BlockSpec((1, n), lambda i: (0, 0))],
        out_specs=pl.BlockSpec((tm, n), lambda i: (i, 0)),
        out_shape=jax.ShapeDtypeStruct((m, n), BF16),
        compiler_params=_cparams(1), name="mla_cq",
    )(xn, w, g.reshape(1, n))


def _ckv_kernel(a_ref, w_ref, g_ref, cos_ref, sin_ref, ckv_ref, kpe_ref, *, n_kv):
    acc = _dot(a_ref[...], w_ref[...])
    ckv_ref[...] = _rms(acc[:, :n_kv], g_ref[...]).astype(ckv_ref.dtype)
    kpe_ref[...] = _rot_half(acc[:, n_kv:], cos_ref[...], sin_ref[...]).astype(kpe_ref.dtype)


def ckv_matmul(xn, w, g, cos_p, sin_p, seq, tm=512):
    m, k = xn.shape
    n = w.shape[1]
    n_kv = n - LANES
    nsb = seq // tm
    tab = pl.BlockSpec((tm, LANES), lambda i: (i % nsb, 0))
    return pl.pallas_call(
        functools.partial(_ckv_kernel, n_kv=n_kv), grid=(m // tm,),
        in_specs=[pl.BlockSpec((tm, k), lambda i: (i, 0)),
                  pl.BlockSpec((k, n), lambda i: (0, 0)),
                  pl.BlockSpec((1, n_kv), lambda i: (0, 0)), tab, tab],
        out_specs=[pl.BlockSpec((tm, n_kv), lambda i: (i, 0)),
                   pl.BlockSpec((tm, LANES), lambda i: (i, 0))],
        out_shape=[jax.ShapeDtypeStruct((m, n_kv), BF16), jax.ShapeDtypeStruct((m, LANES), BF16)],
        compiler_params=_cparams(1), name="mla_ckv",
    )(xn, w, g.reshape(1, n_kv), cos_p, sin_p)


def _qup_kernel(a_ref, w_ref, cos_ref, sin_ref, o_ref, *, tn, scale):
    acc = _dot(a_ref[...], w_ref[...])
    cos = cos_ref[...]
    sin = sin_ref[...]
    for c in range(tn // MLA_QK):
        lo = c * MLA_QK
        o_ref[:, lo:lo + LANES] = (acc[:, lo:lo + LANES] * scale).astype(o_ref.dtype)
        pe = _rot_half(acc[:, lo + LANES:lo + MLA_QK], cos, sin) * scale
        o_ref[:, lo + LANES:lo + MLA_QK] = pe.astype(o_ref.dtype)


def qup_matmul(cq, w, cos_p, sin_p, seq, tm=1024, tn=512):
    m, k = cq.shape
    n = w.shape[1]
    nsb = seq // tm
    tab = pl.BlockSpec((tm, LANES), lambda i, j: (i % nsb, 0))
    return pl.pallas_call(
        functools.partial(_qup_kernel, tn=tn, scale=(MLA_NOPE + MLA_ROPE) ** -0.5),
        grid=(m // tm, n // tn),
        in_specs=[pl.BlockSpec((tm, k), lambda i, j: (i, 0)),
                  pl.BlockSpec((k, tn), lambda i, j: (0, j)), tab, tab],
        out_specs=pl.BlockSpec((tm, tn), lambda i, j: (i, j)),
        out_shape=jax.ShapeDtypeStruct((m, n), BF16),
        compiler_params=_cparams(2), name="mla_qup",
    )(cq, w, cos_p, sin_p)


def _swa_kernel(q_ref, kc_ref, kp_ref, vc_ref, vp_ref, o_ref, lse_ref, *, heads):
    n = pl.program_id(2)
    blk = LOCAL_BLOCK
    qi = lax.broadcasted_iota(jnp.int32, (blk, blk), 0)
    kj = lax.broadcasted_iota(jnp.int32, (blk, blk), 1)
    mask_prev = kj >= qi + jnp.where(n > 0, 0, blk)
    mask_cur = kj <= qi
    for h in range(heads):
        sl = slice(h * SWA_HD, (h + 1) * SWA_HD)
        q = q_ref[:, sl]
        s_p = jnp.where(mask_prev, _dot_nt(q, kp_ref[:, sl]), NEG_INF)
        s_c = jnp.where(mask_cur, _dot_nt(q, kc_ref[:, sl]), NEG_INF)
        m = jnp.maximum(jnp.max(s_p, axis=-1, keepdims=True), jnp.max(s_c, axis=-1, keepdims=True))
        p_p = jnp.exp(s_p - m)
        p_c = jnp.exp(s_c - m)
        l = jnp.sum(p_p, axis=-1, keepdims=True) + jnp.sum(p_c, axis=-1, keepdims=True)
        o = _dot(p_p.astype(BF16), vp_ref[:, sl]) + _dot(p_c.astype(BF16), vc_ref[:, sl])
        o_ref[:, sl] = o / l
        lse_ref[:, sl] = jnp.broadcast_to(m + jnp.log(l), (blk, SWA_HD))


def swa_group(qkv, batch, seq, group, dilation, heads):
    w = heads * SWA_HD
    ncol = qkv.shape[1] // w
    L = seq // dilation
    nb = L // LOCAL_BLOCK
    x = qkv.reshape(batch, L, dilation * ncol * w)
    kcol, vcol = ncol - 2, ncol - 1

    def spec(col, prev):
        if prev:
            return pl.BlockSpec((None, LOCAL_BLOCK, w),
                                lambda b, r, n: (b, jnp.maximum(n - 1, 0), r * ncol + col))
        return pl.BlockSpec((None, LOCAL_BLOCK, w), lambda b, r, n: (b, n, r * ncol + col))

    out_spec = pl.BlockSpec((None, LOCAL_BLOCK, w), lambda b, r, n: (b, n, r))
    out_sds = jax.ShapeDtypeStruct((batch, L, dilation * w), F32)
    o, lse = pl.pallas_call(
        functools.partial(_swa_kernel, heads=heads),
        grid=(batch, dilation, nb),
        in_specs=[spec(group, False), spec(kcol, False), spec(kcol, True),
                  spec(vcol, False), spec(vcol, True)],
        out_specs=[out_spec, out_spec],
        out_shape=[out_sds, out_sds],
        compiler_params=_cparams(3), name=f"swa_d{dilation}",
    )(x, x, x, x, x)
    return o.reshape(batch * seq, w), lse.reshape(batch * seq, w)


def _swa_combine_kernel(o0, o1, o2, l0, l1, l2, y_ref):
    a0, a1, a2 = l0[...], l1[...], l2[...]
    m = jnp.maximum(jnp.maximum(a0, a1), a2)
    e0, e1, e2 = jnp.exp(a0 - m), jnp.exp(a1 - m), jnp.exp(a2 - m)
    y = (e0 * o0[...] + e1 * o1[...] + e2 * o2[...]) / (e0 + e1 + e2)
    y_ref[...] = y.astype(y_ref.dtype)


def swa_combine(outs, lses, tm=256):
    m, w = outs[0].shape
    spec = pl.BlockSpec((tm, w), lambda i: (i, 0))
    return pl.pallas_call(
        _swa_combine_kernel, grid=(m // tm,),
        in_specs=[spec] * 6, out_specs=spec,
        out_shape=jax.ShapeDtypeStruct((m, w), BF16),
        compiler_params=_cparams(1), name="swa_combine",
    )(*outs, *lses)


def _conf_kernel(x_ref, halo_ref, w_ref, b_ref, g_ref, beta_ref, y_ref, xs_ref, *, ts, halo):
    i = pl.program_id(1)
    xs_ref[halo:, :] = x_ref[...]
    hal = halo_ref[...]
    xs_ref[:halo, :] = jnp.where(i > 0, hal, jnp.zeros_like(hal))
    acc = jnp.zeros(x_ref.shape, F32) + b_ref[...]
    off = halo - (CONV_K - 1)
    for k in range(CONV_K):
        acc = acc + xs_ref[off + k:off + k + ts, :] * w_ref[k:k + 1, :]
    mu = jnp.mean(acc, axis=-1, keepdims=True)
    d = acc - mu
    var = jnp.mean(d * d, axis=-1, keepdims=True)
    y = d * lax.rsqrt(var + NORM_EPS) * g_ref[...] + beta_ref[...]
    y_ref[...] = _silu(y).astype(y_ref.dtype)


def conformer_conv(glu, w, b, g, beta, batch, seq, ts=256, halo=32):
    c = glu.shape[-1]
    x = glu.reshape(batch, seq, c)
    r = ts // halo
    vec = pl.BlockSpec((1, c), lambda bb, i: (0, 0))
    y = pl.pallas_call(
        functools.partial(_conf_kernel, ts=ts, halo=halo),
        grid=(batch, seq // ts),
        in_specs=[pl.BlockSpec((None, ts, c), lambda bb, i: (bb, i, 0)),
                  pl.BlockSpec((None, halo, c), lambda bb, i: (bb, jnp.maximum(i * r - 1, 0), 0)),
                  pl.BlockSpec((CONV_K, c), lambda bb, i: (0, 0)), vec, vec, vec],
        out_specs=pl.BlockSpec((None, ts, c), lambda bb, i: (bb, i, 0)),
        out_shape=jax.ShapeDtypeStruct((batch, seq, c), BF16),
        scratch_shapes=[pltpu.VMEM((ts + halo, c), F32)],
        compiler_params=_cparams(2), name="conformer_conv",
    )(x, x, w, b.reshape(1, c), g.reshape(1, c), beta.reshape(1, c))
    return y.reshape(batch * seq, c)


def _mla_kernel(q_ref, kn_ref, kpe_ref, v_ref, o_ref, m_ref, l_ref, acc_ref, *, tq):
    qi = pl.program_id(2)
    q_n = q_ref[:, :LANES]
    q_p = q_ref[:, LANES:]
    m_ref[...] = jnp.full(m_ref.shape, NEG_INF, F32)
    l_ref[...] = jnp.zeros(l_ref.shape, F32)
    acc_ref[...] = jnp.zeros(acc_ref.shape, F32)

    def step(kb, masked):
        start = pl.multiple_of(kb * tq, tq)
        s = _dot_nt(q_n, kn_ref[pl.ds(start, tq), :]) + _dot_nt(q_p, kpe_ref[pl.ds(start, tq), :])
        if masked:
            r = lax.broadcasted_iota(jnp.int32, (tq, tq), 0)
            c = lax.broadcasted_iota(jnp.int32, (tq, tq), 1)
            s = jnp.where(c <= r, s, NEG_INF)
        m_prev = m_ref[...]
        m_new = jnp.maximum(m_prev, jnp.max(s, axis=-1, keepdims=True))
        alpha = jnp.exp(m_prev - m_new)
        p = jnp.exp(s - m_new[:, :1])
        l_ref[...] = alpha * l_ref[...] + jnp.sum(p, axis=-1, keepdims=True)
        acc_ref[...] = alpha * acc_ref[...] + _dot(p.astype(BF16), v_ref[pl.ds(start, tq), :])
        m_ref[...] = m_new

    def body(kb, carry):
        step(kb, False)
        return carry

    lax.fori_loop(0, qi, body, 0)
    step(qi, True)
    o_ref[...] = (acc_ref[...] / l_ref[...]).astype(o_ref.dtype)


def mla_attention(q, kv, kpe, batch, seq, heads, tq=512):
    q3 = q.reshape(batch, seq, heads * MLA_QK)
    kv3 = kv.reshape(batch, seq, heads * (MLA_NOPE + MLA_V))
    kpe3 = kpe.reshape(batch, seq, LANES)
    stat = pltpu.VMEM((tq, LANES), F32)
    y = pl.pallas_call(
        functools.partial(_mla_kernel, tq=tq),
        grid=(batch, heads, seq // tq),
        in_specs=[pl.BlockSpec((None, tq, MLA_QK), lambda b, h, i: (b, i, h)),
                  pl.BlockSpec((None, seq, LANES), lambda b, h, i: (b, 0, 2 * h)),
                  pl.BlockSpec((None, seq, LANES), lambda b, h, i: (b, 0, 0)),
                  pl.BlockSpec((None, seq, LANES), lambda b, h, i: (b, 0, 2 * h + 1))],
        out_specs=pl.BlockSpec((None, tq, MLA_V), lambda b, h, i: (b, i, h)),
        out_shape=jax.ShapeDtypeStruct((batch, seq, heads * MLA_V), BF16),
        scratch_shapes=[stat, stat, pltpu.VMEM((tq, MLA_V), F32)],
        compiler_params=_cparams(3), name="mla_attn",
    )(q3, kv3, kpe3, kv3)
    return y.reshape(batch * seq, heads * MLA_V)


def _lru_gates_kernel(x_ref, halo_ref, wc_ref, bc_ref, wa_ref, ba_ref, wx_ref, bx_ref, lam_ref,
                      a_ref, b_ref, xs_ref, *, ts, halo, bw):
    i = pl.program_id(1)
    xs_ref[halo:, :] = x_ref[...]
    hal = halo_ref[...]
    xs_ref[:halo, :] = jnp.where(i > 0, hal, jnp.zeros_like(hal))
    u = jnp.zeros(x_ref.shape, F32) + bc_ref[...]
    off = halo - (LRU_CONV_K - 1)
    for k in range(LRU_CONV_K):
        u = u + xs_ref[off + k:off + k + ts, :] * wc_ref[k:k + 1, :]
    lam = lam_ref[...]
    neg = -lam
    softplus = jnp.maximum(neg, 0.0) + jnp.log1p(jnp.exp(-jnp.abs(neg)))
    for n in range(LRU_BLOCKS):
        sl = slice(n * bw, (n + 1) * bw)
        ub = u[:, sl]
        ubf = ub.astype(BF16)
        r = _sigmoid(_dot(ubf, wa_ref[n]) + ba_ref[:, sl])
        g = _sigmoid(_dot(ubf, wx_ref[n]) + bx_ref[:, sl])
        log_a = (-LRU_C) * r * softplus[:, sl]
        a_ref[:, sl] = jnp.exp(log_a)
        th = jnp.tanh(log_a)
        b_ref[:, sl] = jnp.sqrt(-2.0 * th / (1.0 - th)) * (g * ub)


def lru_gates(lru_in, wc, bc, wa, ba, wx, bx, lam, batch, seq, ts=256, halo=8):
    c = lru_in.shape[-1]
    bw = c // LRU_BLOCKS
    x = lru_in.reshape(batch, seq, c)
    r = ts // halo
    vec = pl.BlockSpec((1, c), lambda bb, i: (0, 0))
    blk = pl.BlockSpec((None, ts, c), lambda bb, i: (bb, i, 0))
    wspec = pl.BlockSpec((LRU_BLOCKS, bw, bw), lambda bb, i: (0, 0, 0))
    sds = jax.ShapeDtypeStruct((batch, seq, c), F32)
    return pl.pallas_call(
        functools.partial(_lru_gates_kernel, ts=ts, halo=halo, bw=bw),
        grid=(batch, seq // ts),
        in_specs=[blk,
                  pl.BlockSpec((None, halo, c), lambda bb, i: (bb, jnp.maximum(i * r - 1, 0), 0)),
                  pl.BlockSpec((LRU_CONV_K, c), lambda bb, i: (0, 0)), vec,
                  wspec, vec, wspec, vec, vec],
        out_specs=[blk, blk], out_shape=[sds, sds],
        scratch_shapes=[pltpu.VMEM((ts + halo, c), F32)],
        compiler_params=_cparams(2), name="lru_gates",
    )(x, x, wc, bc.reshape(1, c), wa, ba.reshape(1, c), wx, bx.reshape(1, c), lam.reshape(1, c))


def _lru_scan_kernel(a_ref, b_ref, gate_ref, y_ref, h_ref, hs_ref, *, ts):
    @pl.when(pl.program_id(1) == 0)
    def _():
        h_ref[...] = jnp.zeros(h_ref.shape, F32)

    def body(t, h):
        h = a_ref[pl.ds(t, 1), :] * h + b_ref[pl.ds(t, 1), :]
        hs_ref[pl.ds(t, 1), :] = h
        return h

    h_ref[...] = lax.fori_loop(0, ts, body, h_ref[...], unroll=8)
    y_ref[...] = (hs_ref[...] * gate_ref[...]).astype(y_ref.dtype)


def lru_scan(a, b, gate, batch, seq, ts=256):
    c = a.shape[-1]
    blk = pl.BlockSpec((None, ts, c), lambda bb, i: (bb, i, 0))
    y = pl.pallas_call(
        functools.partial(_lru_scan_kernel, ts=ts),
        grid=(batch, seq // ts),
        in_specs=[blk, blk, blk], out_specs=blk,
        out_shape=jax.ShapeDtypeStruct((batch, seq, c), BF16),
        scratch_shapes=[pltpu.VMEM((1, c), F32), pltpu.VMEM((ts, c), F32)],
        compiler_params=_cparams(2), name="lru_scan",
    )(a, b, gate.reshape(batch, seq, c))
    return y.reshape(batch * seq, c)


def _merge_kernel(xn_ref, wg_ref, bg_ref, y_ref, wb_ref, o_ref, acc_ref):
    i = pl.program_id(2)
    gate = _sigmoid(_dot(xn_ref[...], wg_ref[...]) + bg_ref[...])
    contrib = gate * _dot(y_ref[...], wb_ref[...])

    @pl.when(i == 0)
    def _():
        acc_ref[...] = contrib

    @pl.when(i > 0)
    def _():
        acc_ref[...] += contrib

    @pl.when(i == pl.num_programs(2) - 1)
    def _():
        o_ref[...] = acc_ref[...].astype(o_ref.dtype)


def gated_merge(xn, w_gate, b_gate, ys, w_bo, tm=1024, tn=512):
    m, d = xn.shape
    nbr, bwid, _ = w_bo.shape
    return pl.pallas_call(
        _merge_kernel,
        grid=(m // tm, d // tn, nbr),
        in_specs=[pl.BlockSpec((tm, d), lambda a, j, i: (a, 0)),
                  pl.BlockSpec((None, d, tn), lambda a, j, i: (i, 0, j)),
                  pl.BlockSpec((None, 1, tn), lambda a, j, i: (i, 0, j)),
                  pl.BlockSpec((None, tm, bwid), lambda a, j, i: (i, a, 0)),
                  pl.BlockSpec((None, bwid, tn), lambda a, j, i: (i, 0, j))],
        out_specs=pl.BlockSpec((tm, tn), lambda a, j, i: (a, j)),
        out_shape=jax.ShapeDtypeStruct((m, d), BF16),
        scratch_shapes=[pltpu.VMEM((tm, tn), F32)],
        compiler_params=_cparams(3), name="gated_merge",
    )(xn, w_gate, b_gate.reshape(nbr, 1, d), ys, w_bo)


def _xattn_kernel(a_ref, w_ref, k_ref, v_ref, o_ref, *, scale):
    q = (_dot(a_ref[...], w_ref[...]) * scale).astype(BF16)
    s = _dot_nt(q, k_ref[...])
    m = jnp.max(s, axis=-1, keepdims=True)
    p = jnp.exp(s - m)
    l = jnp.sum(p, axis=-1, keepdims=True)
    o_ref[...] = (_dot(p.astype(BF16), v_ref[...]) / l).astype(o_ref.dtype)


def cross_attention(hn, w_xq, kx, vx, seq, mem_len, tm=1024):
    m, d = hn.shape
    spb = seq // tm
    kvspec = pl.BlockSpec((mem_len, X_HD), lambda i, h: (i // spb, h))
    return pl.pallas_call(
        functools.partial(_xattn_kernel, scale=X_HD ** -0.5),
        grid=(m // tm, X_HEADS),
        in_specs=[pl.BlockSpec((tm, d), lambda i, h: (i, 0)),
                  pl.BlockSpec((d, X_HD), lambda i, h: (0, h)), kvspec, kvspec],
        out_specs=pl.BlockSpec((tm, X_HD), lambda i, h: (i, h)),
        out_shape=jax.ShapeDtypeStruct((m, X_HEADS * X_HD), BF16),
        compiler_params=_cparams(2), name="cross_attn",
    )(hn, w_xq, kx, vx)


def _rope_tables(seq):
    def tables(dim):
        inv_freq = ROPE_THETA ** (-jnp.arange(0, dim, 2, dtype=F32) / dim)
        ang = jnp.arange(seq, dtype=F32)[:, None] * inv_freq[None, :]
        return jnp.cos(ang), jnp.sin(ang)

    cos_a, sin_a = tables(SWA_HD)
    cos2 = jnp.concatenate([cos_a, cos_a], axis=-1)
    sin2 = jnp.concatenate([-sin_a, sin_a], axis=-1)
    cos_c, sin_c = tables(MLA_ROPE)
    z = jnp.zeros_like(cos_c)
    cos_p = jnp.concatenate([cos_c, z, cos_c, z], axis=-1)
    sin_p = jnp.concatenate([-sin_c, z, sin_c, z], axis=-1)
    return cos2, sin2, cos_p, sin_p


def _pe_layout(w):
    half = MLA_ROPE // 2
    z = jnp.zeros(w.shape[:-1] + (half,), w.dtype)
    return jnp.concatenate([w[..., :half], z, w[..., half:], z], axis=-1)


def kernel(x, mem, g_mix_pre, w_in, w_mla_q_up, g_mla_q, w_mla_kv_up, g_mla_kv, w_conf_dw, b_conf_dw, g_conf_ln, b_conf_ln, w_lru_conv, b_lru_conv, w_lru_a, b_lru_a, w_lru_x, b_lru_x, lru_lambda, w_branch_out, w_gate, b_gate, w_o, g_mix_post, g_x_pre, g_mem, w_xq, w_xk, w_xv, w_xo, g_x_post, g_ffn_pre, w_ffn_gate, w_ffn_up, w_ffn_down, g_ffn_post):
    batch, seq, d = x.shape
    depth = w_in.shape[0]
    mem_len = mem.shape[1]
    bw = d // 4
    heads = bw // SWA_HD
    n_groups = len(SWA_PATTERNS)
    q_lora = w_mla_q_up.shape[1]
    kv_lora = w_mla_kv_up.shape[1]
    t = batch * seq

    cuts = np.cumsum([n_groups * bw, bw, bw, bw, bw, q_lora, kv_lora, MLA_ROPE, bw, bw]).tolist()
    c_q, c_k, c_v, c_cv, c_cg, c_cq, c_ckv, c_kpe, c_li, c_lg = cuts
    cos2, sin2, cos_p, sin_p = _rope_tables(seq)

    h = x.reshape(t, d)
    mem2 = mem.reshape(batch * mem_len, d)
    hn = rms_cast(h, g_mix_pre[0])

    for l in range(depth):
        wl = w_in[l]
        w_qkv = wl[:, :c_v].astype(BF16)
        w_cval = wl[:, c_v:c_cv].astype(BF16)
        w_cgate = wl[:, c_cv:c_cg].astype(BF16)
        w_cq = wl[:, c_cg:c_cq].astype(BF16)
        w_ckv = jnp.concatenate([wl[:, c_cq:c_ckv], _pe_layout(wl[:, c_ckv:c_kpe])], axis=-1).astype(BF16)
        w_lin = wl[:, c_kpe:c_li].astype(BF16)
        w_lgate = wl[:, c_li:c_lg].astype(BF16)
        wq = w_mla_q_up[l].reshape(q_lora, heads, MLA_NOPE + MLA_ROPE)
        w_qup = jnp.concatenate([wq[..., :MLA_NOPE], _pe_layout(wq[..., MLA_NOPE:])], axis=-1)
        w_qup = w_qup.reshape(q_lora, heads * MLA_QK).astype(BF16)

        qkv = qkv_matmul(hn, w_qkv, cos2, sin2, seq, n_groups * bw, bw)
        outs, lses = [], []
        for g, (window, dilation) in enumerate(SWA_PATTERNS):
            assert window // dilation == LOCAL_BLOCK and seq % (dilation * LOCAL_BLOCK) == 0
            o, lse = swa_group(qkv, batch, seq, g, dilation, heads)
            outs.append(o)
            lses.append(lse)
        y_a = swa_combine(outs, lses)

        glu = dual_matmul(hn, w_cval, w_cgate, "glu", F32, 1024, 512, "conf_glu")
        y_b = conformer_conv(glu, w_conf_dw[l], b_conf_dw[l], g_conf_ln[l], b_conf_ln[l], batch, seq)

        cq = cq_matmul(hn, w_cq, g_mla_q[l])
        ckv, kpe = ckv_matmul(hn, w_ckv, g_mla_kv[l], cos_p, sin_p, seq)
        q_c = qup_matmul(cq, w_qup, cos_p, sin_p, seq)
        kv = matmul(ckv, w_mla_kv_up[l].astype(BF16), BF16, 1024, 512, name="mla_kvup")
        y_c = mla_attention(q_c, kv, kpe, batch, seq, heads)

        lru_in = matmul(hn, w_lin, F32, 1024, 512, name="lru_in")
        gate = matmul(hn, w_lgate, F32, 1024, 512, act="gelu", name="lru_gate")
        a, b = lru_gates(lru_in, w_lru_conv[l], b_lru_conv[l], w_lru_a[l].astype(BF16), b_lru_a[l],
                         w_lru_x[l].astype(BF16), b_lru_x[l], lru_lambda[l], batch, seq)
        y_d = lru_scan(a, b, gate, batch, seq)

        ys = jnp.stack([y_a, y_b, y_c, y_d])
        merged = gated_merge(hn, w_gate[l].astype(BF16), b_gate[l], ys, w_branch_out[l].astype(BF16))
        mix = matmul(merged, w_o[l].astype(BF16), F32, 1024, 512, name="w_o")
        h, hn = resid_norm(h, mix, g_mix_post[l], g_x_pre[l])

        mem_n = rms_cast(mem2, g_mem[l])
        tmem = min(512, batch * mem_len)
        kx = matmul(mem_n, w_xk[l].astype(BF16), BF16, tmem, 512, name="x_k")
        vx = matmul(mem_n, w_xv[l].astype(BF16), BF16, tmem, 512, name="x_v")
        xo = cross_attention(hn, w_xq[l].astype(BF16), kx, vx, seq, mem_len)
        xa = matmul(xo, w_xo[l].astype(BF16), F32, 1024, 512, name="x_o")
        h, hn = resid_norm(h, xa, g_x_post[l], g_ffn_pre[l])

        act = dual_matmul(hn, w_ffn_gate[l].astype(BF16), w_ffn_up[l].astype(BF16), "swiglu", BF16,
                          1024, 256, "ffn_up")
        ff = matmul(act, w_ffn_down[l].astype(BF16), F32, 512, 256, name="ffn_down")
        g_next = g_mix_pre[l + 1] if l + 1 < depth else None
        h, hn = resid_norm(h, ff, g_ffn_post[l], g_next)

    return h.reshape(batch, seq, d)
```

```python
import functools

import jax
import jax.numpy as jnp
import numpy as np
from jax import lax
from jax.experimental import pallas as pl
from jax.experimental.pallas import tpu as pltpu

NORM_EPS = 1e-6
ROPE_THETA = 10000.0
NEG_INF = -1e30

SWA_HD = 128
SWA_PATTERNS = ((128, 1), (512, 4), (2048, 16))
LOCAL_BLOCK = 128
CONV_K = 31
MLA_NOPE = 128
MLA_ROPE = 64
MLA_V = 128
MLA_QK = 256
LRU_BLOCKS = 8
LRU_CONV_K = 4
LRU_C = 8.0
X_HEADS = 4
X_HD = 256
LANES = 128
VMEM_LIMIT = 48 * 1024 * 1024

BF16 = jnp.bfloat16
F32 = jnp.float32


def _cparams(n_axes):
    return pltpu.CompilerParams(dimension_semantics=("arbitrary",) * n_axes,
                                vmem_limit_bytes=VMEM_LIMIT)


def _dot(a, b):
    return jnp.dot(a, b, preferred_element_type=F32)


def _dot_nt(a, b):
    return lax.dot_general(a, b, (((1,), (1,)), ((), ())), preferred_element_type=F32)


def _rms(x, g):
    return x * lax.rsqrt(jnp.mean(x * x, axis=-1, keepdims=True) + NORM_EPS) * g


def _sigmoid(x):
    return 1.0 / (1.0 + jnp.exp(-x))


def _silu(x):
    return x * _sigmoid(x)


def _gelu_tanh(x):
    return 0.5 * x * (1.0 + jnp.tanh(np.sqrt(2.0 / np.pi) * (x + 0.044715 * (x * x * x))))


def _rot_half(x, cos, sin):
    return x * cos + pltpu.roll(x, 64, 1) * sin


def _rms_cast_kernel(x_ref, g_ref, o_ref):
    o_ref[...] = _rms(x_ref[...], g_ref[...]).astype(o_ref.dtype)


def rms_cast(x, g, tm=256):
    m, d = x.shape
    return pl.pallas_call(
        _rms_cast_kernel,
        grid=(m // tm,),
        in_specs=[pl.BlockSpec((tm, d), lambda i: (i, 0)),
                  pl.BlockSpec((1, d), lambda i: (0, 0))],
        out_specs=pl.BlockSpec((tm, d), lambda i: (i, 0)),
        out_shape=jax.ShapeDtypeStruct((m, d), BF16),
        compiler_params=_cparams(1),
        name="rms_cast",
    )(x, g.reshape(1, d))


def _resid_norm_kernel(h_ref, y_ref, gpost_ref, gpre_ref, h_out_ref, hn_ref):
    h_new = h_ref[...] + _rms(y_ref[...], gpost_ref[...])
    h_out_ref[...] = h_new
    hn_ref[...] = _rms(h_new, gpre_ref[...]).astype(hn_ref.dtype)


def _resid_kernel(h_ref, y_ref, gpost_ref, h_out_ref):
    h_out_ref[...] = h_ref[...] + _rms(y_ref[...], gpost_ref[...])


def resid_norm(h, y, g_post, g_pre, tm=128):
    m, d = h.shape
    row = pl.BlockSpec((tm, d), lambda i: (i, 0))
    vec = pl.BlockSpec((1, d), lambda i: (0, 0))
    if g_pre is None:
        return pl.pallas_call(
            _resid_kernel, grid=(m // tm,),
            in_specs=[row, row, vec], out_specs=row,
            out_shape=jax.ShapeDtypeStruct((m, d), F32),
            compiler_params=_cparams(1), name="resid",
        )(h, y, g_post.reshape(1, d)), None
    return pl.pallas_call(
        _resid_norm_kernel, grid=(m // tm,),
        in_specs=[row, row, vec, vec], out_specs=[row, row],
        out_shape=[jax.ShapeDtypeStruct((m, d), F32), jax.ShapeDtypeStruct((m, d), BF16)],
        compiler_params=_cparams(1), name="resid_norm",
    )(h, y, g_post.reshape(1, d), g_pre.reshape(1, d))


def _mm_kernel(a_ref, w_ref, o_ref, *, act):
    acc = _dot(a_ref[...], w_ref[...])
    if act == "gelu":
        acc = _gelu_tanh(acc)
    o_ref[...] = acc.astype(o_ref.dtype)


def matmul(a, w, out_dtype, tm, tn, act=None, name="mm"):
    m, k = a.shape
    n = w.shape[1]
    return pl.pallas_call(
        functools.partial(_mm_kernel, act=act),
        grid=(m // tm, n // tn),
        in_specs=[pl.BlockSpec((tm, k), lambda i, j: (i, 0)),
                  pl.BlockSpec((k, tn), lambda i, j: (0, j))],
        out_specs=pl.BlockSpec((tm, tn), lambda i, j: (i, j)),
        out_shape=jax.ShapeDtypeStruct((m, n), out_dtype),
        compiler_params=_cparams(2), name=name,
    )(a, w)


def _dual_kernel(a_ref, w1_ref, w2_ref, o_ref, *, mode):
    a = a_ref[...]
    p1 = _dot(a, w1_ref[...])
    p2 = _dot(a, w2_ref[...])
    if mode == "glu":
        r = p1 * _sigmoid(p2)
    else:
        r = _silu(p1) * p2
    o_ref[...] = r.astype(o_ref.dtype)


def dual_matmul(a, w1, w2, mode, out_dtype, tm, tn, name):
    m, k = a.shape
    n = w1.shape[1]
    wspec = pl.BlockSpec((k, tn), lambda i, j: (0, j))
    return pl.pallas_call(
        functools.partial(_dual_kernel, mode=mode),
        grid=(m // tm, n // tn),
        in_specs=[pl.BlockSpec((tm, k), lambda i, j: (i, 0)), wspec, wspec],
        out_specs=pl.BlockSpec((tm, tn), lambda i, j: (i, j)),
        out_shape=jax.ShapeDtypeStruct((m, n), out_dtype),
        compiler_params=_cparams(2), name=name,
    )(a, w1, w2)


def _qkv_kernel(a_ref, w_ref, cos_ref, sin_ref, o_ref, *, tn, n_q, n_k, scale):
    j = pl.program_id(1)
    acc = _dot(a_ref[...], w_ref[...])

    def rotary(s):
        cos = cos_ref[...]
        sin = sin_ref[...]
        for c in range(tn // SWA_HD):
            sl = slice(c * SWA_HD, (c + 1) * SWA_HD)
            o_ref[:, sl] = (_rot_half(acc[:, sl], cos, sin) * s).astype(o_ref.dtype)

    @pl.when(j < n_q)
    def _():
        rotary(scale)

    @pl.when((j >= n_q) & (j < n_q + n_k))
    def _():
        rotary(1.0)

    @pl.when(j >= n_q + n_k)
    def _():
        o_ref[...] = acc.astype(o_ref.dtype)


def qkv_matmul(xn, w_qkv, cos2, sin2, seq, q_cols, k_cols, tm=1024, tn=512):
    m, k = xn.shape
    n = w_qkv.shape[1]
    nsb = seq // tm
    tab = pl.BlockSpec((tm, SWA_HD), lambda i, j: (i % nsb, 0))
    return pl.pallas_call(
        functools.partial(_qkv_kernel, tn=tn, n_q=q_cols // tn, n_k=k_cols // tn,
                          scale=SWA_HD ** -0.5),
        grid=(m // tm, n // tn),
        in_specs=[pl.BlockSpec((tm, k), lambda i, j: (i, 0)),
                  pl.BlockSpec((k, tn), lambda i, j: (0, j)), tab, tab],
        out_specs=pl.BlockSpec((tm, tn), lambda i, j: (i, j)),
        out_shape=jax.ShapeDtypeStruct((m, n), F32),
        compiler_params=_cparams(2), name="qkv_rope",
    )(xn, w_qkv, cos2, sin2)


def _cq_kernel(a_ref, w_ref, g_ref, o_ref):
    o_ref[...] = _rms(_dot(a_ref[...], w_ref[...]), g_ref[...]).astype(o_ref.dtype)


def cq_matmul(xn, w, g, tm=512):
    m, k = xn.shape
    n = w.shape[1]
    return pl.pallas_call(
        _cq_kernel, grid=(m // tm,),
        in_specs=[pl.BlockSpec((tm, k), lambda i: (i, 0)),
                  pl.BlockSpec((k, n), lambda i: (0, 0)),
                  pl.BlockSpec((1, n), lambda i: (0, 0))],
        out_specs=pl.BlockSpec((tm, n), lambda i: (i, 0)),
        out_shape=jax.ShapeDtypeStruct((m, n), BF16),
        compiler_params=_cparams(1), name="mla_cq",
    )(xn, w, g.reshape(1, n))


def _ckv_kernel(a_ref, w_ref, g_ref, cos_ref, sin_ref, ckv_ref, kpe_ref, *, n_kv):
    acc = _dot(a_ref[...], w_ref[...])
    ckv_ref[...] = _rms(acc[:, :n_kv], g_ref[...]).astype(ckv_ref.dtype)
    kpe_ref[...] = _rot_half(acc[:, n_kv:], cos_ref[...], sin_ref[...]).astype(kpe_ref.dtype)


def ckv_matmul(xn, w, g, cos_p, sin_p, seq, tm=512):
    m, k = xn.shape
    n = w.shape[1]
    n_kv = n - LANES
    nsb = seq // tm
    tab = pl.BlockSpec((tm, LANES), lambda i: (i % nsb, 0))
    return pl.pallas_call(
        functools.partial(_ckv_kernel, n_kv=n_kv), grid=(m // tm,),
        in_specs=[pl.BlockSpec((tm, k), lambda i: (i, 0)),
                  pl.BlockSpec((k, n), lambda i: (0, 0)),
                  pl.BlockSpec((1, n_kv), lambda i: (0, 0)), tab, tab],
        out_specs=[pl.BlockSpec((tm, n_kv), lambda i: (i, 0)),
                   pl.BlockSpec((tm, LANES), lambda i: (i, 0))],
        out_shape=[jax.ShapeDtypeStruct((m, n_kv), BF16), jax.ShapeDtypeStruct((m, LANES), BF16)],
        compiler_params=_cparams(1), name="mla_ckv",
    )(xn, w, g.reshape(1, n_kv), cos_p, sin_p)


def _qup_kernel(a_ref, w_ref, cos_ref, sin_ref, o_ref, *, tn, scale):
    acc = _dot(a_ref[...], w_ref[...])
    cos = cos_ref[...]
    sin = sin_ref[...]
    for c in range(tn // MLA_QK):
        lo = c * MLA_QK
        o_ref[:, lo:lo + LANES] = (acc[:, lo:lo + LANES] * scale).astype(o_ref.dtype)
        pe = _rot_half(acc[:, lo + LANES:lo + MLA_QK], cos, sin) * scale
        o_ref[:, lo + LANES:lo + MLA_QK] = pe.astype(o_ref.dtype)


def qup_matmul(cq, w, cos_p, sin_p, seq, tm=1024, tn=512):
    m, k = cq.shape
    n = w.shape[1]
    nsb = seq // tm
    tab = pl.BlockSpec((tm, LANES), lambda i, j: (i % nsb, 0))
    return pl.pallas_call(
        functools.partial(_qup_kernel, tn=tn, scale=(MLA_NOPE + MLA_ROPE) ** -0.5 * np.log2(np.e)),
        grid=(m // tm, n // tn),
        in_specs=[pl.BlockSpec((tm, k), lambda i, j: (i, 0)),
                  pl.BlockSpec((k, tn), lambda i, j: (0, j)), tab, tab],
        out_specs=pl.BlockSpec((tm, tn), lambda i, j: (i, j)),
        out_shape=jax.ShapeDtypeStruct((m, n), BF16),
        compiler_params=_cparams(2), name="mla_qup",
    )(cq, w, cos_p, sin_p)


def _swa_kernel(q0_ref, q1_ref, q2_ref, kc_ref, kp_ref, vc_ref, vp_ref, y_ref, o_sc, lse_sc, *, rows):
    n = pl.program_id(1)
    blk = LOCAL_BLOCK
    qi = lax.broadcasted_iota(jnp.int32, (blk, 2 * blk), 0)
    c = lax.broadcasted_iota(jnp.int32, (blk, 2 * blk), 1)
    band = jnp.where(c <= qi + blk, jnp.where(c >= qi, 0.0, NEG_INF), NEG_INF)
    lo = jnp.maximum(qi, jnp.where(n > 0, 0, blk))
    first = jnp.where(c <= qi + blk, jnp.where(c >= lo, 0.0, NEG_INF), NEG_INF)
    q_refs = (q0_ref, q1_ref, q2_ref)
    for g, (_, d) in enumerate(SWA_PATTERNS):
        span = d * blk
        for s in range(rows // span):
            for r in range(d):
                cur = pl.ds(s * span + r, blk, stride=d)
                q = q_refs[g][cur, :].astype(BF16)
                if s > 0:
                    prev = pl.ds((s - 1) * span + r, blk, stride=d)
                    k_prev, v_prev, bias = kc_ref[prev, :], vc_ref[prev, :], band
                else:
                    prev = pl.ds(rows - span + r, blk, stride=d)
                    k_prev, v_prev, bias = kp_ref[prev, :], vp_ref[prev, :], first
                k2 = jnp.concatenate([k_prev, kc_ref[cur, :]], axis=0).astype(BF16)
                v2 = jnp.concatenate([v_prev, vc_ref[cur, :]], axis=0).astype(BF16)
                sc = _dot_nt(q, k2) + bias
                m = jnp.max(sc, axis=-1, keepdims=True)
                p = jnp.exp(sc - m)
                l = jnp.sum(p, axis=-1, keepdims=True)
                o_sc[g, cur, :] = _dot(p.astype(BF16), v2) / l
                lse_sc[g, cur, :] = jnp.broadcast_to(m + jnp.log(l), (blk, SWA_HD))
    a0, a1, a2 = lse_sc[0], lse_sc[1], lse_sc[2]
    m = jnp.maximum(jnp.maximum(a0, a1), a2)
    e0, e1, e2 = jnp.exp(a0 - m), jnp.exp(a1 - m), jnp.exp(a2 - m)
    y = (e0 * o_sc[0] + e1 * o_sc[1] + e2 * o_sc[2]) / (e0 + e1 + e2)
    y_ref[...] = y.astype(y_ref.dtype)


def swa_attention(qkv, batch, seq, heads):
    w = heads * SWA_HD
    n_groups = len(SWA_PATTERNS)
    rows = max(d for _, d in SWA_PATTERNS) * LOCAL_BLOCK
    assert seq % rows == 0 and all(win // d == LOCAL_BLOCK for win, d in SWA_PATTERNS)
    x = qkv.reshape(batch, seq, (n_groups + 2) * w)

    def spec(col, prev=False):
        if prev:
            return pl.BlockSpec((None, rows, SWA_HD),
                                lambda b, n, h: (b, jnp.maximum(n - 1, 0), col * heads + h))
        return pl.BlockSpec((None, rows, SWA_HD), lambda b, n, h: (b, n, col * heads + h))

    kcol, vcol = n_groups, n_groups + 1
    sc = pltpu.VMEM((n_groups, rows, SWA_HD), F32)
    y = pl.pallas_call(
        functools.partial(_swa_kernel, rows=rows),
        grid=(batch, seq // rows, heads),
        in_specs=[spec(0), spec(1), spec(2), spec(kcol), spec(kcol, True), spec(vcol), spec(vcol, True)],
        out_specs=pl.BlockSpec((None, rows, SWA_HD), lambda b, n, h: (b, n, h)),
        out_shape=jax.ShapeDtypeStruct((batch, seq, w), BF16),
        scratch_shapes=[sc, sc],
        compiler_params=_cparams(3), name="swa_attn",
    )(x, x, x, x, x, x, x)
    return y.reshape(batch * seq, w)


def _conf_kernel(x_ref, halo_ref, w_ref, b_ref, g_ref, beta_ref, y_ref, xs_ref, *, ts, halo):
    i = pl.program_id(1)
    xs_ref[halo:, :] = x_ref[...]
    hal = halo_ref[...]
    xs_ref[:halo, :] = jnp.where(i > 0, hal, jnp.zeros_like(hal))
    acc = jnp.zeros(x_ref.shape, F32) + b_ref[...]
    off = halo - (CONV_K - 1)
    for k in range(CONV_K):
        acc = acc + xs_ref[off + k:off + k + ts, :] * w_ref[k:k + 1, :]
    mu = jnp.mean(acc, axis=-1, keepdims=True)
    d = acc - mu
    var = jnp.mean(d * d, axis=-1, keepdims=True)
    y = d * lax.rsqrt(var + NORM_EPS) * g_ref[...] + beta_ref[...]
    y_ref[...] = _silu(y).astype(y_ref.dtype)


def conformer_conv(glu, w, b, g, beta, batch, seq, ts=256, halo=32):
    c = glu.shape[-1]
    x = glu.reshape(batch, seq, c)
    r = ts // halo
    vec = pl.BlockSpec((1, c), lambda bb, i: (0, 0))
    y = pl.pallas_call(
        functools.partial(_conf_kernel, ts=ts, halo=halo),
        grid=(batch, seq // ts),
        in_specs=[pl.BlockSpec((None, ts, c), lambda bb, i: (bb, i, 0)),
                  pl.BlockSpec((None, halo, c), lambda bb, i: (bb, jnp.maximum(i * r - 1, 0), 0)),
                  pl.BlockSpec((CONV_K, c), lambda bb, i: (0, 0)), vec, vec, vec],
        out_specs=pl.BlockSpec((None, ts, c), lambda bb, i: (bb, i, 0)),
        out_shape=jax.ShapeDtypeStruct((batch, seq, c), BF16),
        scratch_shapes=[pltpu.VMEM((ts + halo, c), F32)],
        compiler_params=_cparams(2), name="conformer_conv",
    )(x, x, w, b.reshape(1, c), g.reshape(1, c), beta.reshape(1, c))
    return y.reshape(batch * seq, c)


def _kvup_kernel(c_ref, wk_ref, wv_ref, kpe_ref, k_ref, v_ref, *, heads):
    c = c_ref[...]
    k = _dot(c, wk_ref[...])
    kpe = kpe_ref[...]
    for h in range(heads):
        k_ref[:, h * MLA_QK:h * MLA_QK + MLA_NOPE] = k[:, h * MLA_NOPE:(h + 1) * MLA_NOPE].astype(k_ref.dtype)
        k_ref[:, h * MLA_QK + MLA_NOPE:(h + 1) * MLA_QK] = kpe
    v_ref[...] = _dot(c, wv_ref[...]).astype(v_ref.dtype)


def kvup_matmul(ckv, w_k, w_v, kpe, heads, tm=512):
    m, k = ckv.shape
    full = lambda n: pl.BlockSpec((k, n), lambda i: (0, 0))
    row = lambda n: pl.BlockSpec((tm, n), lambda i: (i, 0))
    return pl.pallas_call(
        functools.partial(_kvup_kernel, heads=heads), grid=(m // tm,),
        in_specs=[row(k), full(heads * MLA_NOPE), full(heads * MLA_V), row(LANES)],
        out_specs=[row(heads * MLA_QK), row(heads * MLA_V)],
        out_shape=[jax.ShapeDtypeStruct((m, heads * MLA_QK), BF16),
                   jax.ShapeDtypeStruct((m, heads * MLA_V), BF16)],
        compiler_params=_cparams(1), name="mla_kvup",
    )(ckv, w_k, w_v, kpe)


def _mla_kernel(q_ref, k_ref, v_ref, o_ref, m_ref, l_ref, acc_ref, *, tq):
    qi = pl.program_id(2)
    q = q_ref[...]
    m_ref[...] = jnp.full(m_ref.shape, NEG_INF, F32)
    l_ref[...] = jnp.zeros(l_ref.shape, F32)
    acc_ref[...] = jnp.zeros(acc_ref.shape, F32)
    nch = tq // LANES

    def step(kb, masked):
        start = pl.multiple_of(kb * tq, tq)
        s = _dot_nt(q, k_ref[pl.ds(start, tq), :])
        if masked:
            r = lax.broadcasted_iota(jnp.int32, (tq, tq), 0)
            c = lax.broadcasted_iota(jnp.int32, (tq, tq), 1)
            s = jnp.where(c <= r, s, NEG_INF)
        chunks = [s[:, j * LANES:(j + 1) * LANES] for j in range(nch)]
        mx = functools.reduce(jnp.maximum, chunks)
        m_prev = m_ref[...]
        m_new = jnp.maximum(m_prev, jnp.max(mx, axis=-1, keepdims=True))
        alpha = jnp.exp2(m_prev - m_new)
        ps = [jnp.exp2(ch - m_new) for ch in chunks]
        l_ref[...] = alpha * l_ref[...] + functools.reduce(jnp.add, ps)
        p = jnp.concatenate([x.astype(BF16) for x in ps], axis=1)
        acc_ref[...] = alpha * acc_ref[...] + _dot(p, v_ref[pl.ds(start, tq), :])
        m_ref[...] = m_new

    def body(kb, carry):
        step(kb, False)
        return carry

    lax.fori_loop(0, qi, body, 0)
    step(qi, True)
    l = jnp.sum(l_ref[...], axis=-1, keepdims=True)
    o_ref[...] = (acc_ref[...] / l).astype(o_ref.dtype)


def mla_attention(q, k, v, batch, seq, heads, tq=1024):
    q3 = q.reshape(batch, seq, heads * MLA_QK)
    k3 = k.reshape(batch, seq, heads * MLA_QK)
    v3 = v.reshape(batch, seq, heads * MLA_V)
    stat = pltpu.VMEM((tq, LANES), F32)
    y = pl.pallas_call(
        functools.partial(_mla_kernel, tq=tq),
        grid=(batch, heads, seq // tq),
        in_specs=[pl.BlockSpec((None, tq, MLA_QK), lambda b, h, i: (b, i, h)),
                  pl.BlockSpec((None, seq, MLA_QK), lambda b, h, i: (b, 0, h)),
                  pl.BlockSpec((None, seq, MLA_V), lambda b, h, i: (b, 0, h))],
        out_specs=pl.BlockSpec((None, tq, MLA_V), lambda b, h, i: (b, i, h)),
        out_shape=jax.ShapeDtypeStruct((batch, seq, heads * MLA_V), BF16),
        scratch_shapes=[stat, stat, pltpu.VMEM((tq, MLA_V), F32)],
        compiler_params=_cparams(3), name="mla_attn",
    )(q3, k3, v3)
    return y.reshape(batch * seq, heads * MLA_V)


def _lru_gates_kernel(x_ref, halo_ref, wc_ref, bc_ref, wa_ref, ba_ref, wx_ref, bx_ref, lam_ref,
                      a_ref, b_ref, xs_ref, *, ts, halo, bw):
    i = pl.program_id(1)
    xs_ref[halo:, :] = x_ref[...]
    hal = halo_ref[...]
    xs_ref[:halo, :] = jnp.where(i > 0, hal, jnp.zeros_like(hal))
    u = jnp.zeros(x_ref.shape, F32) + bc_ref[...]
    off = halo - (LRU_CONV_K - 1)
    for k in range(LRU_CONV_K):
        u = u + xs_ref[off + k:off + k + ts, :] * wc_ref[k:k + 1, :]
    lam = lam_ref[...]
    neg = -lam
    softplus = jnp.maximum(neg, 0.0) + jnp.log1p(jnp.exp(-jnp.abs(neg)))
    for n in range(LRU_BLOCKS):
        sl = slice(n * bw, (n + 1) * bw)
        ub = u[:, sl]
        ubf = ub.astype(BF16)
        r = _sigmoid(_dot(ubf, wa_ref[n]) + ba_ref[:, sl])
        g = _sigmoid(_dot(ubf, wx_ref[n]) + bx_ref[:, sl])
        log_a = (-LRU_C) * r * softplus[:, sl]
        a_ref[:, sl] = jnp.exp(log_a)
        th = jnp.tanh(log_a)
        b_ref[:, sl] = jnp.sqrt(-2.0 * th / (1.0 - th)) * (g * ub)


def lru_gates(lru_in, wc, bc, wa, ba, wx, bx, lam, batch, seq, ts=256, halo=8):
    c = lru_in.shape[-1]
    bw = c // LRU_BLOCKS
    x = lru_in.reshape(batch, seq, c)
    r = ts // halo
    vec = pl.BlockSpec((1, c), lambda bb, i: (0, 0))
    blk = pl.BlockSpec((None, ts, c), lambda bb, i: (bb, i, 0))
    wspec = pl.BlockSpec((LRU_BLOCKS, bw, bw), lambda bb, i: (0, 0, 0))
    sds = jax.ShapeDtypeStruct((batch, seq, c), F32)
    return pl.pallas_call(
        functools.partial(_lru_gates_kernel, ts=ts, halo=halo, bw=bw),
        grid=(batch, seq // ts),
        in_specs=[blk,
                  pl.BlockSpec((None, halo, c), lambda bb, i: (bb, jnp.maximum(i * r - 1, 0), 0)),
                  pl.BlockSpec((LRU_CONV_K, c), lambda bb, i: (0, 0)), vec,
                  wspec, vec, wspec, vec, vec],
        out_specs=[blk, blk], out_shape=[sds, sds],
        scratch_shapes=[pltpu.VMEM((ts + halo, c), F32)],
        compiler_params=_cparams(2), name="lru_gates",
    )(x, x, wc, bc.reshape(1, c), wa, ba.reshape(1, c), wx, bx.reshape(1, c), lam.reshape(1, c))


def _lru_scan_kernel(a_ref, b_ref, gate_ref, y_ref, h_ref, hs_ref, *, ts):
    @pl.when(pl.program_id(1) == 0)
    def _():
        h_ref[...] = jnp.zeros(h_ref.shape, F32)

    def body(t, h):
        h = a_ref[pl.ds(t, 1), :] * h + b_ref[pl.ds(t, 1), :]
        hs_ref[pl.ds(t, 1), :] = h
        return h

    h_ref[...] = lax.fori_loop(0, ts, body, h_ref[...], unroll=8)
    y_ref[...] = (hs_ref[...] * gate_ref[...]).astype(y_ref.dtype)


def lru_scan(a, b, gate, batch, seq, ts=256):
    c = a.shape[-1]
    blk = pl.BlockSpec((None, ts, c), lambda bb, i: (bb, i, 0))
    y = pl.pallas_call(
        functools.partial(_lru_scan_kernel, ts=ts),
        grid=(batch, seq // ts),
        in_specs=[blk, blk, blk], out_specs=blk,
        out_shape=jax.ShapeDtypeStruct((batch, seq, c), BF16),
        scratch_shapes=[pltpu.VMEM((1, c), F32), pltpu.VMEM((ts, c), F32)],
        compiler_params=_cparams(2), name="lru_scan",
    )(a, b, gate.reshape(batch, seq, c))
    return y.reshape(batch * seq, c)


def _merge_kernel(xn_ref, wg_ref, bg_ref, y_ref, wb_ref, o_ref, acc_ref):
    i = pl.program_id(2)
    gate = _sigmoid(_dot(xn_ref[...], wg_ref[...]) + bg_ref[...])
    contrib = gate * _dot(y_ref[...], wb_ref[...])

    @pl.when(i == 0)
    def _():
        acc_ref[...] = contrib

    @pl.when(i > 0)
    def _():
        acc_ref[...] += contrib

    @pl.when(i == pl.num_programs(2) - 1)
    def _():
        o_ref[...] = acc_ref[...].astype(o_ref.dtype)


def gated_merge(xn, w_gate, b_gate, ys, w_bo, tm=1024, tn=512):
    m, d = xn.shape
    nbr, bwid, _ = w_bo.shape
    return pl.pallas_call(
        _merge_kernel,
        grid=(m // tm, d // tn, nbr),
        in_specs=[pl.BlockSpec((tm, d), lambda a, j, i: (a, 0)),
                  pl.BlockSpec((None, d, tn), lambda a, j, i: (i, 0, j)),
                  pl.BlockSpec((None, 1, tn), lambda a, j, i: (i, 0, j)),
                  pl.BlockSpec((None, tm, bwid), lambda a, j, i: (i, a, 0)),
                  pl.BlockSpec((None, bwid, tn), lambda a, j, i: (i, 0, j))],
        out_specs=pl.BlockSpec((tm, tn), lambda a, j, i: (a, j)),
        out_shape=jax.ShapeDtypeStruct((m, d), BF16),
        scratch_shapes=[pltpu.VMEM((tm, tn), F32)],
        compiler_params=_cparams(3), name="gated_merge",
    )(xn, w_gate, b_gate.reshape(nbr, 1, d), ys, w_bo)


def _xattn_kernel(a_ref, w_ref, k_ref, v_ref, o_ref, *, scale):
    q = (_dot(a_ref[...], w_ref[...]) * scale).astype(BF16)
    s = _dot_nt(q, k_ref[...])
    m = jnp.max(s, axis=-1, keepdims=True)
    p = jnp.exp(s - m)
    l = jnp.sum(p, axis=-1, keepdims=True)
    o_ref[...] = (_dot(p.astype(BF16), v_ref[...]) / l).astype(o_ref.dtype)


def cross_attention(hn, w_xq, kx, vx, seq, mem_len, tm=1024):
    m, d = hn.shape
    spb = seq // tm
    kvspec = pl.BlockSpec((mem_len, X_HD), lambda i, h: (i // spb, h))
    return pl.pallas_call(
        functools.partial(_xattn_kernel, scale=X_HD ** -0.5),
        grid=(m // tm, X_HEADS),
        in_specs=[pl.BlockSpec((tm, d), lambda i, h: (i, 0)),
                  pl.BlockSpec((d, X_HD), lambda i, h: (0, h)), kvspec, kvspec],
        out_specs=pl.BlockSpec((tm, X_HD), lambda i, h: (i, h)),
        out_shape=jax.ShapeDtypeStruct((m, X_HEADS * X_HD), BF16),
        compiler_params=_cparams(2), name="cross_attn",
    )(hn, w_xq, kx, vx)


def _rope_tables(seq):
    def tables(dim):
        inv_freq = ROPE_THETA ** (-jnp.arange(0, dim, 2, dtype=F32) / dim)
        ang = jnp.arange(seq, dtype=F32)[:, None] * inv_freq[None, :]
        return jnp.cos(ang), jnp.sin(ang)

    cos_a, sin_a = tables(SWA_HD)
    cos2 = jnp.concatenate([cos_a, cos_a], axis=-1)
    sin2 = jnp.concatenate([-sin_a, sin_a], axis=-1)
    cos_c, sin_c = tables(MLA_ROPE)
    z = jnp.zeros_like(cos_c)
    cos_p = jnp.concatenate([cos_c, z, cos_c, z], axis=-1)
    sin_p = jnp.concatenate([-sin_c, z, sin_c, z], axis=-1)
    return cos2, sin2, cos_p, sin_p


def _pe_layout(w):
    half = MLA_ROPE // 2
    z = jnp.zeros(w.shape[:-1] + (half,), w.dtype)
    return jnp.concatenate([w[..., :half], z, w[..., half:], z], axis=-1)


def kernel(x, mem, g_mix_pre, w_in, w_mla_q_up, g_mla_q, w_mla_kv_up, g_mla_kv, w_conf_dw, b_conf_dw, g_conf_ln, b_conf_ln, w_lru_conv, b_lru_conv, w_lru_a, b_lru_a, w_lru_x, b_lru_x, lru_lambda, w_branch_out, w_gate, b_gate, w_o, g_mix_post, g_x_pre, g_mem, w_xq, w_xk, w_xv, w_xo, g_x_post, g_ffn_pre, w_ffn_gate, w_ffn_up, w_ffn_down, g_ffn_post):
    batch, seq, d = x.shape
    depth = w_in.shape[0]
    mem_len = mem.shape[1]
    bw = d // 4
    heads = bw // SWA_HD
    n_groups = len(SWA_PATTERNS)
    q_lora = w_mla_q_up.shape[1]
    kv_lora = w_mla_kv_up.shape[1]
    t = batch * seq

    cuts = np.cumsum([n_groups * bw, bw, bw, bw, bw, q_lora, kv_lora, MLA_ROPE, bw, bw]).tolist()
    c_q, c_k, c_v, c_cv, c_cg, c_cq, c_ckv, c_kpe, c_li, c_lg = cuts
    cos2, sin2, cos_p, sin_p = _rope_tables(seq)

    h = x.reshape(t, d)
    mem2 = mem.reshape(batch * mem_len, d)
    hn = rms_cast(h, g_mix_pre[0])

    for l in range(depth):
        wl = w_in[l]
        w_qkv = wl[:, :c_v].astype(BF16)
        w_cval = wl[:, c_v:c_cv].astype(BF16)
        w_cgate = wl[:, c_cv:c_cg].astype(BF16)
        w_cq = wl[:, c_cg:c_cq].astype(BF16)
        w_ckv = jnp.concatenate([wl[:, c_cq:c_ckv], _pe_layout(wl[:, c_ckv:c_kpe])], axis=-1).astype(BF16)
        w_lin = wl[:, c_kpe:c_li].astype(BF16)
        w_lgate = wl[:, c_li:c_lg].astype(BF16)
        wq = w_mla_q_up[l].reshape(q_lora, heads, MLA_NOPE + MLA_ROPE)
        w_qup = jnp.concatenate([wq[..., :MLA_NOPE], _pe_layout(wq[..., MLA_NOPE:])], axis=-1)
        w_qup = w_qup.reshape(q_lora, heads * MLA_QK).astype(BF16)

        qkv = qkv_matmul(hn, w_qkv, cos2, sin2, seq, n_groups * bw, bw)
        y_a = swa_attention(qkv, batch, seq, heads)

        glu = dual_matmul(hn, w_cval, w_cgate, "glu", F32, 1024, 512, "conf_glu")
        y_b = conformer_conv(glu, w_conf_dw[l], b_conf_dw[l], g_conf_ln[l], b_conf_ln[l], batch, seq)

        cq = cq_matmul(hn, w_cq, g_mla_q[l])
        ckv, kpe = ckv_matmul(hn, w_ckv, g_mla_kv[l], cos_p, sin_p, seq)
        q_c = qup_matmul(cq, w_qup, cos_p, sin_p, seq)
        wkv = w_mla_kv_up[l].reshape(kv_lora, heads, MLA_NOPE + MLA_V)
        w_kup = wkv[..., :MLA_NOPE].reshape(kv_lora, heads * MLA_NOPE).astype(BF16)
        w_vup = wkv[..., MLA_NOPE:].reshape(kv_lora, heads * MLA_V).astype(BF16)
        k_c, v_c = kvup_matmul(ckv, w_kup, w_vup, kpe, heads)
        y_c = mla_attention(q_c, k_c, v_c, batch, seq, heads)

        lru_in = matmul(hn, w_lin, F32, 1024, 512, name="lru_in")
        gate = matmul(hn, w_lgate, F32, 1024, 512, act="gelu", name="lru_gate")
        a, b = lru_gates(lru_in, w_lru_conv[l], b_lru_conv[l], w_lru_a[l].astype(BF16), b_lru_a[l],
                         w_lru_x[l].astype(BF16), b_lru_x[l], lru_lambda[l], batch, seq)
        y_d = lru_scan(a, b, gate, batch, seq)

        ys = jnp.stack([y_a, y_b, y_c, y_d])
        merged = gated_merge(hn, w_gate[l].astype(BF16), b_gate[l], ys, w_branch_out[l].astype(BF16))
        mix = matmul(merged, w_o[l].astype(BF16), F32, 1024, 512, name="w_o")
        h, hn = resid_norm(h, mix, g_mix_post[l], g_x_pre[l])

        mem_n = rms_cast(mem2, g_mem[l])
        tmem = min(512, batch * mem_len)
        kx = matmul(mem_n, w_xk[l].astype(BF16), BF16, tmem, 512, name="x_k")
        vx = matmul(mem_n, w_xv[l].astype(BF16), BF16, tmem, 512, name="x_v")
        xo = cross_attention(hn, w_xq[l].astype(BF16), kx, vx, seq, mem_len)
        xa = matmul(xo, w_xo[l].astype(BF16), F32, 1024, 512, name="x_o")
        h, hn = resid_norm(h, xa, g_x_post[l], g_ffn_pre[l])

        act = dual_matmul(hn, w_ffn_gate[l].astype(BF16), w_ffn_up[l].astype(BF16), "swiglu", BF16,
                          1024, 256, "ffn_up")
        ff = matmul(act, w_ffn_down[l].astype(BF16), F32, 512, 256, name="ffn_down")
        g_next = g_mix_pre[l + 1] if l + 1 < depth else None
        h, hn = resid_norm(h, ff, g_ffn_post[l], g_next)

    return h.reshape(batch, seq, d)
```

```python
import functools

import jax
import jax.numpy as jnp
import numpy as np
from jax import lax
from jax.experimental import pallas as pl
from jax.experimental.pallas import tpu as pltpu

NORM_EPS = 1e-6
ROPE_THETA = 10000.0
NEG_INF = -1e30

SWA_HD = 128
SWA_PATTERNS = ((128, 1), (512, 4), (2048, 16))
LOCAL_BLOCK = 128
CONV_K = 31
MLA_NOPE = 128
MLA_ROPE = 64
MLA_V = 128
MLA_QK = 256
LRU_BLOCKS = 8
LRU_CONV_K = 4
LRU_C = 8.0
X_HEADS = 4
X_HD = 256
LANES = 128
VMEM_LIMIT = 48 * 1024 * 1024

BF16 = jnp.bfloat16
F32 = jnp.float32


def _cparams(n_axes):
    return pltpu.CompilerParams(dimension_semantics=("arbitrary",) * n_axes,
                                vmem_limit_bytes=VMEM_LIMIT)


def _dot(a, b):
    return jnp.dot(a, b, preferred_element_type=F32)


def _dot_nt(a, b):
    return lax.dot_general(a, b, (((1,), (1,)), ((), ())), preferred_element_type=F32)


def _rms(x, g):
    return x * lax.rsqrt(jnp.mean(x * x, axis=-1, keepdims=True) + NORM_EPS) * g


def _sigmoid(x):
    return 1.0 / (1.0 + jnp.exp(-x))


def _silu(x):
    return x * _sigmoid(x)


def _gelu_tanh(x):
    return 0.5 * x * (1.0 + jnp.tanh(np.sqrt(2.0 / np.pi) * (x + 0.044715 * (x * x * x))))


def _rot_half(x, cos, sin):
    return x * cos + pltpu.roll(x, 64, 1) * sin


def _w_spec(w, layer, tn, col_of):
    k = w.shape[-2]
    if w.ndim == 3:
        return pl.BlockSpec((None, k, tn), lambda *g: (layer, 0, col_of(*g)))
    return pl.BlockSpec((k, tn), lambda *g: (0, col_of(*g)))


def _rms_cast_kernel(x_ref, g_ref, o_ref):
    o_ref[...] = _rms(x_ref[...], g_ref[...]).astype(o_ref.dtype)


def rms_cast(x, g, tm=256):
    m, d = x.shape
    return pl.pallas_call(
        _rms_cast_kernel,
        grid=(m // tm,),
        in_specs=[pl.BlockSpec((tm, d), lambda i: (i, 0)),
                  pl.BlockSpec((1, d), lambda i: (0, 0))],
        out_specs=pl.BlockSpec((tm, d), lambda i: (i, 0)),
        out_shape=jax.ShapeDtypeStruct((m, d), BF16),
        compiler_params=_cparams(1),
        name="rms_cast",
    )(x, g.reshape(1, d))


def _resid_norm_kernel(h_ref, y_ref, gpost_ref, gpre_ref, h_out_ref, hn_ref):
    h_new = h_ref[...] + _rms(y_ref[...], gpost_ref[...])
    h_out_ref[...] = h_new
    hn_ref[...] = _rms(h_new, gpre_ref[...]).astype(hn_ref.dtype)


def _resid_kernel(h_ref, y_ref, gpost_ref, h_out_ref):
    h_out_ref[...] = h_ref[...] + _rms(y_ref[...], gpost_ref[...])


def resid_norm(h, y, g_post, g_pre, tm=128):
    m, d = h.shape
    row = pl.BlockSpec((tm, d), lambda i: (i, 0))
    vec = pl.BlockSpec((1, d), lambda i: (0, 0))
    if g_pre is None:
        return pl.pallas_call(
            _resid_kernel, grid=(m // tm,),
            in_specs=[row, row, vec], out_specs=row,
            out_shape=jax.ShapeDtypeStruct((m, d), F32),
            compiler_params=_cparams(1), name="resid",
        )(h, y, g_post.reshape(1, d)), None
    return pl.pallas_call(
        _resid_norm_kernel, grid=(m // tm,),
        in_specs=[row, row, vec, vec], out_specs=[row, row],
        out_shape=[jax.ShapeDtypeStruct((m, d), F32), jax.ShapeDtypeStruct((m, d), BF16)],
        compiler_params=_cparams(1), name="resid_norm",
    )(h, y, g_post.reshape(1, d), g_pre.reshape(1, d))


def _mm_resid_norm_kernel(a_ref, w_ref, h_ref, gpost_ref, gpre_ref, h_out_ref, hn_ref):
    h_new = h_ref[...] + _rms(_dot(a_ref[...], w_ref[...]), gpost_ref[...])
    h_out_ref[...] = h_new
    hn_ref[...] = _rms(h_new, gpre_ref[...]).astype(hn_ref.dtype)


def matmul_resid_norm(a, w, h, g_post, g_pre, layer=None, tm=128):
    m, k = a.shape
    d = w.shape[-1]
    row = pl.BlockSpec((tm, d), lambda i: (i, 0))
    vec = pl.BlockSpec((1, d), lambda i: (0, 0))
    return pl.pallas_call(
        _mm_resid_norm_kernel, grid=(m // tm,),
        in_specs=[pl.BlockSpec((tm, k), lambda i: (i, 0)), _w_spec(w, layer, d, lambda i: 0),
                  row, vec, vec],
        out_specs=[row, row],
        out_shape=[jax.ShapeDtypeStruct((m, d), F32), jax.ShapeDtypeStruct((m, d), BF16)],
        compiler_params=_cparams(1), name="mm_resid_norm",
    )(a, w, h, g_post.reshape(1, d), g_pre.reshape(1, d))


def _mm_kernel(a_ref, w_ref, o_ref, *, act):
    acc = _dot(a_ref[...], w_ref[...])
    if act == "gelu":
        acc = _gelu_tanh(acc)
    o_ref[...] = acc.astype(o_ref.dtype)


def matmul(a, w, out_dtype, tm, tn, act=None, layer=None, name="mm"):
    m, k = a.shape
    n = w.shape[-1]
    return pl.pallas_call(
        functools.partial(_mm_kernel, act=act),
        grid=(m // tm, n // tn),
        in_specs=[pl.BlockSpec((tm, k), lambda i, j: (i, 0)), _w_spec(w, layer, tn, lambda i, j: j)],
        out_specs=pl.BlockSpec((tm, tn), lambda i, j: (i, j)),
        out_shape=jax.ShapeDtypeStruct((m, n), out_dtype),
        compiler_params=_cparams(2), name=name,
    )(a, w)


def _dual_kernel(a_ref, w1_ref, w2_ref, o_ref, *, mode):
    a = a_ref[...]
    p1 = _dot(a, w1_ref[...])
    p2 = _dot(a, w2_ref[...])
    if mode == "glu":
        r = p1 * _sigmoid(p2)
    else:
        r = _silu(p1) * p2
    o_ref[...] = r.astype(o_ref.dtype)


def dual_matmul(a, w1, w2, mode, out_dtype, tm, tn, name, layer=None):
    m, k = a.shape
    n = w1.shape[-1]
    return pl.pallas_call(
        functools.partial(_dual_kernel, mode=mode),
        grid=(m // tm, n // tn),
        in_specs=[pl.BlockSpec((tm, k), lambda i, j: (i, 0)),
                  _w_spec(w1, layer, tn, lambda i, j: j), _w_spec(w2, layer, tn, lambda i, j: j)],
        out_specs=pl.BlockSpec((tm, tn), lambda i, j: (i, j)),
        out_shape=jax.ShapeDtypeStruct((m, n), out_dtype),
        compiler_params=_cparams(2), name=name,
    )(a, w1, w2)


def _qkv_kernel(a_ref, w_ref, cos_ref, sin_ref, o_ref, *, tn, n_q, n_k, scale):
    j = pl.program_id(1)
    acc = _dot(a_ref[...], w_ref[...])

    def rotary(s):
        cos = cos_ref[...]
        sin = sin_ref[...]
        for c in range(tn // SWA_HD):
            sl = slice(c * SWA_HD, (c + 1) * SWA_HD)
            o_ref[:, sl] = (_rot_half(acc[:, sl], cos, sin) * s).astype(o_ref.dtype)

    @pl.when(j < n_q)
    def _():
        rotary(scale)

    @pl.when((j >= n_q) & (j < n_q + n_k))
    def _():
        rotary(1.0)

    @pl.when(j >= n_q + n_k)
    def _():
        o_ref[...] = acc.astype(o_ref.dtype)


def qkv_matmul(xn, w_qkv, cos2, sin2, seq, q_cols, k_cols, tm=1024, tn=512):
    m, k = xn.shape
    n = w_qkv.shape[1]
    nsb = seq // tm
    tab = pl.BlockSpec((tm, SWA_HD), lambda i, j: (i % nsb, 0))
    return pl.pallas_call(
        functools.partial(_qkv_kernel, tn=tn, n_q=q_cols // tn, n_k=k_cols // tn,
                          scale=SWA_HD ** -0.5),
        grid=(m // tm, n // tn),
        in_specs=[pl.BlockSpec((tm, k), lambda i, j: (i, 0)),
                  pl.BlockSpec((k, tn), lambda i, j: (0, j)), tab, tab],
        out_specs=pl.BlockSpec((tm, tn), lambda i, j: (i, j)),
        out_shape=jax.ShapeDtypeStruct((m, n), F32),
        compiler_params=_cparams(2), name="qkv_rope",
    )(xn, w_qkv, cos2, sin2)


def _cq_kernel(a_ref, w_ref, g_ref, o_ref):
    o_ref[...] = _rms(_dot(a_ref[...], w_ref[...]), g_ref[...]).astype(o_ref.dtype)


def cq_matmul(xn, w, g, tm=512):
    m, k = xn.shape
    n = w.shape[1]
    return pl.pallas_call(
        _cq_kernel, grid=(m // tm,),
        in_specs=[pl.BlockSpec((tm, k), lambda i: (i, 0)),
                  pl.BlockSpec((k, n), lambda i: (0, 0)),
                  pl.BlockSpec((1, n), lambda i: (0, 0))],
        out_specs=pl.BlockSpec((tm, n), lambda i: (i, 0)),
        out_shape=jax.ShapeDtypeStruct((m, n), BF16),
        compiler_params=_cparams(1), name="mla_cq",
    )(xn, w, g.reshape(1, n))


def _ckv_kernel(a_ref, w_ref, g_ref, cos_ref, sin_ref, ckv_ref, kpe_ref, *, n_kv):
    acc = _dot(a_ref[...], w_ref[...])
    ckv_ref[...] = _rms(acc[:, :n_kv], g_ref[...]).astype(ckv_ref.dtype)
    kpe_ref[...] = _rot_half(acc[:, n_kv:], cos_ref[...], sin_ref[...]).astype(kpe_ref.dtype)


def ckv_matmul(xn, w, g, cos_p, sin_p, seq, tm=512):
    m, k = xn.shape
    n = w.shape[1]
    n_kv = n - LANES
    nsb = seq // tm
    tab = pl.BlockSpec((tm, LANES), lambda i: (i % nsb, 0))
    return pl.pallas_call(
        functools.partial(_ckv_kernel, n_kv=n_kv), grid=(m // tm,),
        in_specs=[pl.BlockSpec((tm, k), lambda i: (i, 0)),
                  pl.BlockSpec((k, n), lambda i: (0, 0)),
                  pl.BlockSpec((1, n_kv), lambda i: (0, 0)), tab, tab],
        out_specs=[pl.BlockSpec((tm, n_kv), lambda i: (i, 0)),
                   pl.BlockSpec((tm, LANES), lambda i: (i, 0))],
        out_shape=[jax.ShapeDtypeStruct((m, n_kv), BF16), jax.ShapeDtypeStruct((m, LANES), BF16)],
        compiler_params=_cparams(1), name="mla_ckv",
    )(xn, w, g.reshape(1, n_kv), cos_p, sin_p)


def _qup_kernel(a_ref, w_ref, cos_ref, sin_ref, o_ref, *, tn, scale):
    acc = _dot(a_ref[...], w_ref[...])
    cos = cos_ref[...]
    sin = sin_ref[...]
    for c in range(tn // MLA_QK):
        lo = c * MLA_QK
        o_ref[:, lo:lo + LANES] = (acc[:, lo:lo + LANES] * scale).astype(o_ref.dtype)
        pe = _rot_half(acc[:, lo + LANES:lo + MLA_QK], cos, sin) * scale
        o_ref[:, lo + LANES:lo + MLA_QK] = pe.astype(o_ref.dtype)


def qup_matmul(cq, w, cos_p, sin_p, seq, tm=1024, tn=512):
    m, k = cq.shape
    n = w.shape[1]
    nsb = seq // tm
    tab = pl.BlockSpec((tm, LANES), lambda i, j: (i % nsb, 0))
    return pl.pallas_call(
        functools.partial(_qup_kernel, tn=tn, scale=(MLA_NOPE + MLA_ROPE) ** -0.5 * np.log2(np.e)),
        grid=(m // tm, n // tn),
        in_specs=[pl.BlockSpec((tm, k), lambda i, j: (i, 0)),
                  pl.BlockSpec((k, tn), lambda i, j: (0, j)), tab, tab],
        out_specs=pl.BlockSpec((tm, tn), lambda i, j: (i, j)),
        out_shape=jax.ShapeDtypeStruct((m, n), BF16),
        compiler_params=_cparams(2), name="mla_qup",
    )(cq, w, cos_p, sin_p)


def _swa_kernel(q0_ref, q1_ref, q2_ref, kc_ref, kp_ref, vc_ref, vp_ref, y_ref, o_sc, lse_sc, *, rows):
    n = pl.program_id(1)
    blk = LOCAL_BLOCK
    qi = lax.broadcasted_iota(jnp.int32, (blk, 2 * blk), 0)
    c = lax.broadcasted_iota(jnp.int32, (blk, 2 * blk), 1)
    band = jnp.where(c <= qi + blk, jnp.where(c >= qi, 0.0, NEG_INF), NEG_INF)
    lo = jnp.maximum(qi, jnp.where(n > 0, 0, blk))
    first = jnp.where(c <= qi + blk, jnp.where(c >= lo, 0.0, NEG_INF), NEG_INF)
    q_refs = (q0_ref, q1_ref, q2_ref)
    for g, (_, d) in enumerate(SWA_PATTERNS):
        span = d * blk
        for s in range(rows // span):
            for r in range(d):
                cur = pl.ds(s * span + r, blk, stride=d)
                q = q_refs[g][cur, :].astype(BF16)
                if s > 0:
                    prev = pl.ds((s - 1) * span + r, blk, stride=d)
                    k_prev, v_prev, bias = kc_ref[prev, :], vc_ref[prev, :], band
                else:
                    prev = pl.ds(rows - span + r, blk, stride=d)
                    k_prev, v_prev, bias = kp_ref[prev, :], vp_ref[prev, :], first
                k2 = jnp.concatenate([k_prev, kc_ref[cur, :]], axis=0).astype(BF16)
                v2 = jnp.concatenate([v_prev, vc_ref[cur, :]], axis=0).astype(BF16)
                sc = _dot_nt(q, k2) + bias
                m = jnp.max(sc, axis=-1, keepdims=True)
                p = jnp.exp(sc - m)
                l = jnp.sum(p, axis=-1, keepdims=True)
                o_sc[g, cur, :] = _dot(p.astype(BF16), v2) / l
                lse_sc[g, cur, :] = jnp.broadcast_to(m + jnp.log(l), (blk, SWA_HD))
    a0, a1, a2 = lse_sc[0], lse_sc[1], lse_sc[2]
    m = jnp.maximum(jnp.maximum(a0, a1), a2)
    e0, e1, e2 = jnp.exp(a0 - m), jnp.exp(a1 - m), jnp.exp(a2 - m)
    y = (e0 * o_sc[0] + e1 * o_sc[1] + e2 * o_sc[2]) / (e0 + e1 + e2)
    y_ref[...] = y.astype(y_ref.dtype)


def swa_attention(qkv, batch, seq, heads):
    w = heads * SWA_HD
    n_groups = len(SWA_PATTERNS)
    rows = max(d for _, d in SWA_PATTERNS) * LOCAL_BLOCK
    assert seq % rows == 0 and all(win // d == LOCAL_BLOCK for win, d in SWA_PATTERNS)
    x = qkv.reshape(batch, seq, (n_groups + 2) * w)

    def spec(col, prev=False):
        if prev:
            return pl.BlockSpec((None, rows, SWA_HD),
                                lambda b, n, h: (b, jnp.maximum(n - 1, 0), col * heads + h))
        return pl.BlockSpec((None, rows, SWA_HD), lambda b, n, h: (b, n, col * heads + h))

    kcol, vcol = n_groups, n_groups + 1
    sc = pltpu.VMEM((n_groups, rows, SWA_HD), F32)
    y = pl.pallas_call(
        functools.partial(_swa_kernel, rows=rows),
        grid=(batch, seq // rows, heads),
        in_specs=[spec(0), spec(1), spec(2), spec(kcol), spec(kcol, True), spec(vcol), spec(vcol, True)],
        out_specs=pl.BlockSpec((None, rows, SWA_HD), lambda b, n, h: (b, n, h)),
        out_shape=jax.ShapeDtypeStruct((batch, seq, w), BF16),
        scratch_shapes=[sc, sc],
        compiler_params=_cparams(3), name="swa_attn",
    )(x, x, x, x, x, x, x)
    return y.reshape(batch * seq, w)


def _conf_kernel(x_ref, halo_ref, w_ref, b_ref, g_ref, beta_ref, y_ref, xs_ref, sh_ref, *, ts, halo):
    i = pl.program_id(1)
    xs_ref[halo:, :] = x_ref[...]
    hal = halo_ref[...]
    xs_ref[:halo, :] = jnp.where(i > 0, hal, jnp.zeros_like(hal))
    acc = jnp.zeros(x_ref.shape, F32) + b_ref[...]
    off = halo - (CONV_K - 1)
    for rho in range(8):
        jmax = (CONV_K - 1 - rho) // 8
        rows = ts + 8 * jmax
        sh_ref[:rows, :] = xs_ref[off + rho:off + rho + rows, :]
        for j in range(jmax + 1):
            k = rho + 8 * j
            acc = acc + sh_ref[8 * j:8 * j + ts, :] * w_ref[k:k + 1, :]
    mu = jnp.mean(acc, axis=-1, keepdims=True)
    d = acc - mu
    var = jnp.mean(d * d, axis=-1, keepdims=True)
    y = d * lax.rsqrt(var + NORM_EPS) * g_ref[...] + beta_ref[...]
    y_ref[...] = _silu(y).astype(y_ref.dtype)


def conformer_conv(glu, w, b, g, beta, batch, seq, ts=256, halo=32):
    c = glu.shape[-1]
    x = glu.reshape(batch, seq, c)
    r = ts // halo
    vec = pl.BlockSpec((1, c), lambda bb, i: (0, 0))
    y = pl.pallas_call(
        functools.partial(_conf_kernel, ts=ts, halo=halo),
        grid=(batch, seq // ts),
        in_specs=[pl.BlockSpec((None, ts, c), lambda bb, i: (bb, i, 0)),
                  pl.BlockSpec((None, halo, c), lambda bb, i: (bb, jnp.maximum(i * r - 1, 0), 0)),
                  pl.BlockSpec((CONV_K, c), lambda bb, i: (0, 0)), vec, vec, vec],
        out_specs=pl.BlockSpec((None, ts, c), lambda bb, i: (bb, i, 0)),
        out_shape=jax.ShapeDtypeStruct((batch, seq, c), BF16),
        scratch_shapes=[pltpu.VMEM((ts + halo, c), F32), pltpu.VMEM((ts + halo, c), F32)],
        compiler_params=_cparams(2), name="conformer_conv",
    )(x, x, w, b.reshape(1, c), g.reshape(1, c), beta.reshape(1, c))
    return y.reshape(batch * seq, c)


def _kvup_kernel(c_ref, wk_ref, wv_ref, kpe_ref, k_ref, v_ref, *, heads):
    c = c_ref[...]
    k = _dot(c, wk_ref[...])
    kpe = kpe_ref[...]
    for h in range(heads):
        k_ref[:, h * MLA_QK:h * MLA_QK + MLA_NOPE] = k[:, h * MLA_NOPE:(h + 1) * MLA_NOPE].astype(k_ref.dtype)
        k_ref[:, h * MLA_QK + MLA_NOPE:(h + 1) * MLA_QK] = kpe
    v_ref[...] = _dot(c, wv_ref[...]).astype(v_ref.dtype)


def kvup_matmul(ckv, w_k, w_v, kpe, heads, tm=512):
    m, k = ckv.shape
    full = lambda n: pl.BlockSpec((k, n), lambda i: (0, 0))
    row = lambda n: pl.BlockSpec((tm, n), lambda i: (i, 0))
    return pl.pallas_call(
        functools.partial(_kvup_kernel, heads=heads), grid=(m // tm,),
        in_specs=[row(k), full(heads * MLA_NOPE), full(heads * MLA_V), row(LANES)],
        out_specs=[row(heads * MLA_QK), row(heads * MLA_V)],
        out_shape=[jax.ShapeDtypeStruct((m, heads * MLA_QK), BF16),
                   jax.ShapeDtypeStruct((m, heads * MLA_V), BF16)],
        compiler_params=_cparams(1), name="mla_kvup",
    )(ckv, w_k, w_v, kpe)


def _mla_kernel(q_ref, k_ref, v_ref, o_ref, m_ref, l_ref, acc_ref, *, tq):
    qi = pl.program_id(2)
    q = q_ref[...]
    m_ref[...] = jnp.full(m_ref.shape, NEG_INF, F32)
    l_ref[...] = jnp.zeros(l_ref.shape, F32)
    acc_ref[...] = jnp.zeros(acc_ref.shape, F32)
    nch = tq // LANES

    def step(kb, masked):
        start = pl.multiple_of(kb * tq, tq)
        s = _dot_nt(q, k_ref[pl.ds(start, tq), :])
        if masked:
            r = lax.broadcasted_iota(jnp.int32, (tq, tq), 0)
            c = lax.broadcasted_iota(jnp.int32, (tq, tq), 1)
            s = jnp.where(c <= r, s, NEG_INF)
        chunks = [s[:, j * LANES:(j + 1) * LANES] for j in range(nch)]
        mx = functools.reduce(jnp.maximum, chunks)
        m_prev = m_ref[...]
        m_new = jnp.maximum(m_prev, jnp.max(mx, axis=-1, keepdims=True))
        alpha = jnp.exp2(m_prev - m_new)
        ps = [jnp.exp2(ch - m_new) for ch in chunks]
        l_ref[...] = alpha * l_ref[...] + functools.reduce(jnp.add, ps)
        p = jnp.concatenate([x.astype(BF16) for x in ps], axis=1)
        acc_ref[...] = alpha * acc_ref[...] + _dot(p, v_ref[pl.ds(start, tq), :])
        m_ref[...] = m_new

    def body(kb, carry):
        step(kb, False)
        return carry

    lax.fori_loop(0, qi, body, 0)
    step(qi, True)
    l = jnp.sum(l_ref[...], axis=-1, keepdims=True)
    o_ref[...] = (acc_ref[...] / l).astype(o_ref.dtype)


def mla_attention(q, k, v, batch, seq, heads, tq=1024):
    q3 = q.reshape(batch, seq, heads * MLA_QK)
    k3 = k.reshape(batch, seq, heads * MLA_QK)
    v3 = v.reshape(batch, seq, heads * MLA_V)
    stat = pltpu.VMEM((tq, LANES), F32)
    y = pl.pallas_call(
        functools.partial(_mla_kernel, tq=tq),
        grid=(batch, heads, seq // tq),
        in_specs=[pl.BlockSpec((None, tq, MLA_QK), lambda b, h, i: (b, i, h)),
                  pl.BlockSpec((None, seq, MLA_QK), lambda b, h, i: (b, 0, h)),
                  pl.BlockSpec((None, seq, MLA_V), lambda b, h, i: (b, 0, h))],
        out_specs=pl.BlockSpec((None, tq, MLA_V), lambda b, h, i: (b, i, h)),
        out_shape=jax.ShapeDtypeStruct((batch, seq, heads * MLA_V), BF16),
        scratch_shapes=[stat, stat, pltpu.VMEM((tq, MLA_V), F32)],
        compiler_params=_cparams(3), name="mla_attn",
    )(q3, k3, v3)
    return y.reshape(batch * seq, heads * MLA_V)


def _lru_gates_kernel(x_ref, halo_ref, wc_ref, bc_ref, wa_ref, ba_ref, wx_ref, bx_ref, lam_ref,
                      a_ref, b_ref, xs_ref, *, ts, halo, bw):
    i = pl.program_id(1)
    xs_ref[halo:, :] = x_ref[...]
    hal = halo_ref[...]
    xs_ref[:halo, :] = jnp.where(i > 0, hal, jnp.zeros_like(hal))
    u = jnp.zeros(x_ref.shape, F32) + bc_ref[...]
    off = halo - (LRU_CONV_K - 1)
    for k in range(LRU_CONV_K):
        u = u + xs_ref[off + k:off + k + ts, :] * wc_ref[k:k + 1, :]
    lam = lam_ref[...]
    neg = -lam
    softplus = jnp.maximum(neg, 0.0) + jnp.log1p(jnp.exp(-jnp.abs(neg)))
    for n in range(LRU_BLOCKS):
        sl = slice(n * bw, (n + 1) * bw)
        ub = u[:, sl]
        ubf = ub.astype(BF16)
        r = _sigmoid(_dot(ubf, wa_ref[n]) + ba_ref[:, sl])
        g = _sigmoid(_dot(ubf, wx_ref[n]) + bx_ref[:, sl])
        log_a = (-LRU_C) * r * softplus[:, sl]
        a_ref[:, sl] = jnp.exp(log_a)
        th = jnp.tanh(log_a)
        b_ref[:, sl] = jnp.sqrt(-2.0 * th / (1.0 - th)) * (g * ub)


def lru_gates(lru_in, wc, bc, wa, ba, wx, bx, lam, batch, seq, ts=256, halo=8):
    c = lru_in.shape[-1]
    bw = c // LRU_BLOCKS
    x = lru_in.reshape(batch, seq, c)
    r = ts // halo
    vec = pl.BlockSpec((1, c), lambda bb, i: (0, 0))
    blk = pl.BlockSpec((None, ts, c), lambda bb, i: (bb, i, 0))
    wspec = pl.BlockSpec((LRU_BLOCKS, bw, bw), lambda bb, i: (0, 0, 0))
    sds = jax.ShapeDtypeStruct((batch, seq, c), F32)
    return pl.pallas_call(
        functools.partial(_lru_gates_kernel, ts=ts, halo=halo, bw=bw),
        grid=(batch, seq // ts),
        in_specs=[blk,
                  pl.BlockSpec((None, halo, c), lambda bb, i: (bb, jnp.maximum(i * r - 1, 0), 0)),
                  pl.BlockSpec((LRU_CONV_K, c), lambda bb, i: (0, 0)), vec,
                  wspec, vec, wspec, vec, vec],
        out_specs=[blk, blk], out_shape=[sds, sds],
        scratch_shapes=[pltpu.VMEM((ts + halo, c), F32)],
        compiler_params=_cparams(2), name="lru_gates",
    )(x, x, wc, bc.reshape(1, c), wa, ba.reshape(1, c), wx, bx.reshape(1, c), lam.reshape(1, c))


def _lru_scan_kernel(a_ref, b_ref, gate_ref, y_ref, h_ref, hs_ref, *, ts):
    @pl.when(pl.program_id(1) == 0)
    def _():
        h_ref[...] = jnp.zeros(h_ref.shape, F32)

    def body(t, h):
        h = a_ref[pl.ds(t, 1), :] * h + b_ref[pl.ds(t, 1), :]
        hs_ref[pl.ds(t, 1), :] = h
        return h

    h_ref[...] = lax.fori_loop(0, ts, body, h_ref[...], unroll=8)
    y_ref[...] = (hs_ref[...] * gate_ref[...]).astype(y_ref.dtype)


def lru_scan(a, b, gate, batch, seq, ts=256):
    c = a.shape[-1]
    blk = pl.BlockSpec((None, ts, c), lambda bb, i: (bb, i, 0))
    y = pl.pallas_call(
        functools.partial(_lru_scan_kernel, ts=ts),
        grid=(batch, seq // ts),
        in_specs=[blk, blk, blk], out_specs=blk,
        out_shape=jax.ShapeDtypeStruct((batch, seq, c), BF16),
        scratch_shapes=[pltpu.VMEM((1, c), F32), pltpu.VMEM((ts, c), F32)],
        compiler_params=_cparams(2), name="lru_scan",
    )(a, b, gate.reshape(batch, seq, c))
    return y.reshape(batch * seq, c)


def _merge_kernel(xn_ref, wg_ref, bg_ref, y_ref, wb_ref, o_ref, acc_ref):
    i = pl.program_id(2)
    gate = _sigmoid(_dot(xn_ref[...], wg_ref[...]) + bg_ref[...])
    contrib = gate * _dot(y_ref[...], wb_ref[...])

    @pl.when(i == 0)
    def _():
        acc_ref[...] = contrib

    @pl.when(i > 0)
    def _():
        acc_ref[...] += contrib

    @pl.when(i == pl.num_programs(2) - 1)
    def _():
        o_ref[...] = acc_ref[...].astype(o_ref.dtype)


def gated_merge(xn, w_gate, b_gate, ys, w_bo, layer, tm=1024, tn=512):
    m, d = xn.shape
    nlay, nbr, bwid, _ = w_bo.shape
    return pl.pallas_call(
        _merge_kernel,
        grid=(m // tm, d // tn, nbr),
        in_specs=[pl.BlockSpec((tm, d), lambda a, j, i: (a, 0)),
                  pl.BlockSpec((None, None, d, tn), lambda a, j, i: (layer, i, 0, j)),
                  pl.BlockSpec((None, None, 1, tn), lambda a, j, i: (layer, i, 0, j)),
                  pl.BlockSpec((None, tm, bwid), lambda a, j, i: (i, a, 0)),
                  pl.BlockSpec((None, None, bwid, tn), lambda a, j, i: (layer, i, 0, j))],
        out_specs=pl.BlockSpec((tm, tn), lambda a, j, i: (a, j)),
        out_shape=jax.ShapeDtypeStruct((m, d), BF16),
        scratch_shapes=[pltpu.VMEM((tm, tn), F32)],
        compiler_params=_cparams(3), name="gated_merge",
    )(xn, w_gate, b_gate.reshape(nlay, nbr, 1, d), ys, w_bo)


def _xattn_kernel(a_ref, w_ref, k_ref, v_ref, o_ref, *, scale):
    q = (_dot(a_ref[...], w_ref[...]) * scale).astype(BF16)
    s = _dot_nt(q, k_ref[...])
    m = jnp.max(s, axis=-1, keepdims=True)
    p = jnp.exp(s - m)
    l = jnp.sum(p, axis=-1, keepdims=True)
    o_ref[...] = (_dot(p.astype(BF16), v_ref[...]) / l).astype(o_ref.dtype)


def cross_attention(hn, w_xq, kx, vx, seq, mem_len, layer=None, tm=1024):
    m, d = hn.shape
    spb = seq // tm
    kvspec = pl.BlockSpec((mem_len, X_HD), lambda i, h: (i // spb, h))
    return pl.pallas_call(
        functools.partial(_xattn_kernel, scale=X_HD ** -0.5),
        grid=(m // tm, X_HEADS),
        in_specs=[pl.BlockSpec((tm, d), lambda i, h: (i, 0)),
                  _w_spec(w_xq, layer, X_HD, lambda i, h: h), kvspec, kvspec],
        out_specs=pl.BlockSpec((tm, X_HD), lambda i, h: (i, h)),
        out_shape=jax.ShapeDtypeStruct((m, X_HEADS * X_HD), BF16),
        compiler_params=_cparams(2), name="cross_attn",
    )(hn, w_xq, kx, vx)


def _rope_tables(seq):
    def tables(dim):
        inv_freq = ROPE_THETA ** (-jnp.arange(0, dim, 2, dtype=F32) / dim)
        ang = jnp.arange(seq, dtype=F32)[:, None] * inv_freq[None, :]
        return jnp.cos(ang), jnp.sin(ang)

    cos_a, sin_a = tables(SWA_HD)
    cos2 = jnp.concatenate([cos_a, cos_a], axis=-1)
    sin2 = jnp.concatenate([-sin_a, sin_a], axis=-1)
    cos_c, sin_c = tables(MLA_ROPE)
    z = jnp.zeros_like(cos_c)
    cos_p = jnp.concatenate([cos_c, z, cos_c, z], axis=-1)
    sin_p = jnp.concatenate([-sin_c, z, sin_c, z], axis=-1)
    return cos2, sin2, cos_p, sin_p


def _pe_layout(w):
    half = MLA_ROPE // 2
    z = jnp.zeros(w.shape[:-1] + (half,), w.dtype)
    return jnp.concatenate([w[..., :half], z, w[..., half:], z], axis=-1)


def kernel(x, mem, g_mix_pre, w_in, w_mla_q_up, g_mla_q, w_mla_kv_up, g_mla_kv, w_conf_dw, b_conf_dw, g_conf_ln, b_conf_ln, w_lru_conv, b_lru_conv, w_lru_a, b_lru_a, w_lru_x, b_lru_x, lru_lambda, w_branch_out, w_gate, b_gate, w_o, g_mix_post, g_x_pre, g_mem, w_xq, w_xk, w_xv, w_xo, g_x_post, g_ffn_pre, w_ffn_gate, w_ffn_up, w_ffn_down, g_ffn_post):
    batch, seq, d = x.shape
    depth = w_in.shape[0]
    mem_len = mem.shape[1]
    bw = d // 4
    heads = bw // SWA_HD
    n_groups = len(SWA_PATTERNS)
    q_lora = w_mla_q_up.shape[1]
    kv_lora = w_mla_kv_up.shape[1]
    t = batch * seq

    cuts = np.cumsum([n_groups * bw, bw, bw, bw, bw, q_lora, kv_lora, MLA_ROPE, bw, bw]).tolist()
    c_q, c_k, c_v, c_cv, c_cg, c_cq, c_ckv, c_kpe, c_li, c_lg = cuts
    cos2, sin2, cos_p, sin_p = _rope_tables(seq)

    wg_b, wbo_b, wo_b = w_gate.astype(BF16), w_branch_out.astype(BF16), w_o.astype(BF16)
    wxq_b, wxo_b = w_xq.astype(BF16), w_xo.astype(BF16)
    wfg_b, wfu_b, wfd_b = w_ffn_gate.astype(BF16), w_ffn_up.astype(BF16), w_ffn_down.astype(BF16)

    h = x.reshape(t, d)
    mem2 = mem.reshape(batch * mem_len, d)
    hn = rms_cast(h, g_mix_pre[0])

    for l in range(depth):
        wl = w_in[l]
        w_qkv = wl[:, :c_v].astype(BF16)
        w_cval = wl[:, c_v:c_cv].astype(BF16)
        w_cgate = wl[:, c_cv:c_cg].astype(BF16)
        w_cq = wl[:, c_cg:c_cq].astype(BF16)
        w_ckv = jnp.concatenate([wl[:, c_cq:c_ckv], _pe_layout(wl[:, c_ckv:c_kpe])], axis=-1).astype(BF16)
        w_lin = wl[:, c_kpe:c_li].astype(BF16)
        w_lgate = wl[:, c_li:c_lg].astype(BF16)
        wq = w_mla_q_up[l].reshape(q_lora, heads, MLA_NOPE + MLA_ROPE)
        w_qup = jnp.concatenate([wq[..., :MLA_NOPE], _pe_layout(wq[..., MLA_NOPE:])], axis=-1)
        w_qup = w_qup.reshape(q_lora, heads * MLA_QK).astype(BF16)

        qkv = qkv_matmul(hn, w_qkv, cos2, sin2, seq, n_groups * bw, bw)
        y_a = swa_attention(qkv, batch, seq, heads)

        glu = dual_matmul(hn, w_cval, w_cgate, "glu", F32, 1024, 512, "conf_glu")
        y_b = conformer_conv(glu, w_conf_dw[l], b_conf_dw[l], g_conf_ln[l], b_conf_ln[l], batch, seq)

        cq = cq_matmul(hn, w_cq, g_mla_q[l])
        ckv, kpe = ckv_matmul(hn, w_ckv, g_mla_kv[l], cos_p, sin_p, seq)
        q_c = qup_matmul(cq, w_qup, cos_p, sin_p, seq)
        wkv = w_mla_kv_up[l].reshape(kv_lora, heads, MLA_NOPE + MLA_V)
        w_kup = wkv[..., :MLA_NOPE].reshape(kv_lora, heads * MLA_NOPE).astype(BF16)
        w_vup = wkv[..., MLA_NOPE:].reshape(kv_lora, heads * MLA_V).astype(BF16)
        k_c, v_c = kvup_matmul(ckv, w_kup, w_vup, kpe, heads)
        y_c = mla_attention(q_c, k_c, v_c, batch, seq, heads)

        lru_in = matmul(hn, w_lin, F32, 1024, 512, name="lru_in")
        gate = matmul(hn, w_lgate, F32, 1024, 512, act="gelu", name="lru_gate")
        a, b = lru_gates(lru_in, w_lru_conv[l], b_lru_conv[l], w_lru_a[l].astype(BF16), b_lru_a[l],
                         w_lru_x[l].astype(BF16), b_lru_x[l], lru_lambda[l], batch, seq)
        y_d = lru_scan(a, b, gate, batch, seq)

        ys = jnp.stack([y_a, y_b, y_c, y_d])
        merged = gated_merge(hn, wg_b, b_gate, ys, wbo_b, l)
        mix = matmul(merged, wo_b, F32, 1024, 512, layer=l, name="w_o")
        h, hn = resid_norm(h, mix, g_mix_post[l], g_x_pre[l])

        mem_n = rms_cast(mem2, g_mem[l])
        tmem = min(512, batch * mem_len)
        kx = matmul(mem_n, w_xk[l].astype(BF16), BF16, tmem, 512, name="x_k")
        vx = matmul(mem_n, w_xv[l].astype(BF16), BF16, tmem, 512, name="x_v")
        xo = cross_attention(hn, wxq_b, kx, vx, seq, mem_len, layer=l)
        h, hn = matmul_resid_norm(xo, wxo_b, h, g_x_post[l], g_ffn_pre[l], layer=l)

        act = dual_matmul(hn, wfg_b, wfu_b, "swiglu", BF16, 1024, 256, "ffn_up", layer=l)
        ff = matmul(act, wfd_b, F32, 512, 256, layer=l, name="ffn_down")
        g_next = g_mix_pre[l + 1] if l + 1 < depth else None
        h, hn = resid_norm(h, ff, g_ffn_post[l], g_next)

    return h.reshape(batch, seq, d)
```

```python
import functools

import jax
import jax.numpy as jnp
import numpy as np
from jax import lax
from jax.experimental import pallas as pl
from jax.experimental.pallas import tpu as pltpu

NORM_EPS = 1e-6
ROPE_THETA = 10000.0
NEG_INF = -1e30

SWA_HD = 128
SWA_PATTERNS = ((128, 1), (512, 4), (2048, 16))
LOCAL_BLOCK = 128
CONV_K = 31
MLA_NOPE = 128
MLA_ROPE = 64
MLA_V = 128
MLA_QK = 256
LRU_BLOCKS = 8
LRU_CONV_K = 4
LRU_C = 8.0
X_HEADS = 4
X_HD = 256
LANES = 128
VMEM_LIMIT = 48 * 1024 * 1024

BF16 = jnp.bfloat16
F32 = jnp.float32


def _cparams(n_axes):
    return pltpu.CompilerParams(dimension_semantics=("arbitrary",) * n_axes,
                                vmem_limit_bytes=VMEM_LIMIT)


def _dot(a, b):
    return jnp.dot(a, b, preferred_element_type=F32)


def _dot_nt(a, b):
    return lax.dot_general(a, b, (((1,), (1,)), ((), ())), preferred_element_type=F32)


def _rms(x, g):
    return x * lax.rsqrt(jnp.mean(x * x, axis=-1, keepdims=True) + NORM_EPS) * g


def _sigmoid(x):
    return 1.0 / (1.0 + jnp.exp(-x))


def _silu(x):
    return x * _sigmoid(x)


def _gelu_tanh(x):
    return 0.5 * x * (1.0 + jnp.tanh(np.sqrt(2.0 / np.pi) * (x + 0.044715 * (x * x * x))))


def _rot_half(x, cos, sin):
    return x * cos + pltpu.roll(x, 64, 1) * sin


def _w_spec(w, layer, tn, col_of):
    k = w.shape[-2]
    if w.ndim == 3:
        return pl.BlockSpec((None, k, tn), lambda *g: (layer, 0, col_of(*g)))
    return pl.BlockSpec((k, tn), lambda *g: (0, col_of(*g)))


def _rms_cast_kernel(x_ref, g_ref, o_ref):
    o_ref[...] = _rms(x_ref[...], g_ref[...]).astype(o_ref.dtype)


def rms_cast(x, g, tm=256):
    m, d = x.shape
    return pl.pallas_call(
        _rms_cast_kernel,
        grid=(m // tm,),
        in_specs=[pl.BlockSpec((tm, d), lambda i: (i, 0)),
                  pl.BlockSpec((1, d), lambda i: (0, 0))],
        out_specs=pl.BlockSpec((tm, d), lambda i: (i, 0)),
        out_shape=jax.ShapeDtypeStruct((m, d), BF16),
        compiler_params=_cparams(1),
        name="rms_cast",
    )(x, g.reshape(1, d))


def _resid_norm_kernel(h_ref, y_ref, gpost_ref, gpre_ref, h_out_ref, hn_ref):
    h_new = h_ref[...] + _rms(y_ref[...], gpost_ref[...])
    h_out_ref[...] = h_new
    hn_ref[...] = _rms(h_new, gpre_ref[...]).astype(hn_ref.dtype)


def _resid_kernel(h_ref, y_ref, gpost_ref, h_out_ref):
    h_out_ref[...] = h_ref[...] + _rms(y_ref[...], gpost_ref[...])


def resid_norm(h, y, g_post, g_pre, tm=128):
    m, d = h.shape
    row = pl.BlockSpec((tm, d), lambda i: (i, 0))
    vec = pl.BlockSpec((1, d), lambda i: (0, 0))
    if g_pre is None:
        return pl.pallas_call(
            _resid_kernel, grid=(m // tm,),
            in_specs=[row, row, vec], out_specs=row,
            out_shape=jax.ShapeDtypeStruct((m, d), F32),
            compiler_params=_cparams(1), name="resid",
        )(h, y, g_post.reshape(1, d)), None
    return pl.pallas_call(
        _resid_norm_kernel, grid=(m // tm,),
        in_specs=[row, row, vec, vec], out_specs=[row, row],
        out_shape=[jax.ShapeDtypeStruct((m, d), F32), jax.ShapeDtypeStruct((m, d), BF16)],
        compiler_params=_cparams(1), name="resid_norm",
    )(h, y, g_post.reshape(1, d), g_pre.reshape(1, d))


def _mm_resid_norm_kernel(a_ref, w_ref, h_ref, gpost_ref, gpre_ref, h_out_ref, hn_ref):
    h_new = h_ref[...] + _rms(_dot(a_ref[...], w_ref[...]), gpost_ref[...])
    h_out_ref[...] = h_new
    hn_ref[...] = _rms(h_new, gpre_ref[...]).astype(hn_ref.dtype)


def matmul_resid_norm(a, w, h, g_post, g_pre, layer=None, tm=128):
    m, k = a.shape
    d = w.shape[-1]
    row = pl.BlockSpec((tm, d), lambda i: (i, 0))
    vec = pl.BlockSpec((1, d), lambda i: (0, 0))
    return pl.pallas_call(
        _mm_resid_norm_kernel, grid=(m // tm,),
        in_specs=[pl.BlockSpec((tm, k), lambda i: (i, 0)), _w_spec(w, layer, d, lambda i: 0),
                  row, vec, vec],
        out_specs=[row, row],
        out_shape=[jax.ShapeDtypeStruct((m, d), F32), jax.ShapeDtypeStruct((m, d), BF16)],
        compiler_params=_cparams(1), name="mm_resid_norm",
    )(a, w, h, g_post.reshape(1, d), g_pre.reshape(1, d))


def _mm_kernel(a_ref, w_ref, o_ref, *, act):
    acc = _dot(a_ref[...], w_ref[...])
    if act == "gelu":
        acc = _gelu_tanh(acc)
    o_ref[...] = acc.astype(o_ref.dtype)


def matmul(a, w, out_dtype, tm, tn, act=None, layer=None, name="mm"):
    m, k = a.shape
    n = w.shape[-1]
    return pl.pallas_call(
        functools.partial(_mm_kernel, act=act),
        grid=(m // tm, n // tn),
        in_specs=[pl.BlockSpec((tm, k), lambda i, j: (i, 0)), _w_spec(w, layer, tn, lambda i, j: j)],
        out_specs=pl.BlockSpec((tm, tn), lambda i, j: (i, j)),
        out_shape=jax.ShapeDtypeStruct((m, n), out_dtype),
        compiler_params=_cparams(2), name=name,
    )(a, w)


def _dual_kernel(a_ref, w1_ref, w2_ref, o_ref, *, mode):
    a = a_ref[...]
    p1 = _dot(a, w1_ref[...].astype(BF16))
    p2 = _dot(a, w2_ref[...].astype(BF16))
    if mode == "glu":
        r = p1 * _sigmoid(p2)
    else:
        r = _silu(p1) * p2
    o_ref[...] = r.astype(o_ref.dtype)


def dual_matmul(a, w1, w2, mode, out_dtype, tm, tn, name, layer=None):
    m, k = a.shape
    n = w1.shape[-1]
    return pl.pallas_call(
        functools.partial(_dual_kernel, mode=mode),
        grid=(m // tm, n // tn),
        in_specs=[pl.BlockSpec((tm, k), lambda i, j: (i, 0)),
                  _w_spec(w1, layer, tn, lambda i, j: j), _w_spec(w2, layer, tn, lambda i, j: j)],
        out_specs=pl.BlockSpec((tm, tn), lambda i, j: (i, j)),
        out_shape=jax.ShapeDtypeStruct((m, n), out_dtype),
        compiler_params=_cparams(2), name=name,
    )(a, w1, w2)


def _qkv_kernel(a_ref, w_ref, cos_ref, sin_ref, o_ref, *, tn, n_q, n_k, scale):
    j = pl.program_id(1)
    acc = _dot(a_ref[...], w_ref[...])

    def rotary(s):
        cos = cos_ref[...]
        sin = sin_ref[...]
        for c in range(tn // SWA_HD):
            sl = slice(c * SWA_HD, (c + 1) * SWA_HD)
            o_ref[:, sl] = (_rot_half(acc[:, sl], cos, sin) * s).astype(o_ref.dtype)

    @pl.when(j < n_q)
    def _():
        rotary(scale)

    @pl.when((j >= n_q) & (j < n_q + n_k))
    def _():
        rotary(1.0)

    @pl.when(j >= n_q + n_k)
    def _():
        o_ref[...] = acc.astype(o_ref.dtype)


def qkv_matmul(xn, w_qkv, cos2, sin2, seq, q_cols, k_cols, tm=1024, tn=512):
    m, k = xn.shape
    n = w_qkv.shape[1]
    nsb = seq // tm
    tab = pl.BlockSpec((tm, SWA_HD), lambda i, j: (i % nsb, 0))
    return pl.pallas_call(
        functools.partial(_qkv_kernel, tn=tn, n_q=q_cols // tn, n_k=k_cols // tn,
                          scale=SWA_HD ** -0.5),
        grid=(m // tm, n // tn),
        in_specs=[pl.BlockSpec((tm, k), lambda i, j: (i, 0)),
                  pl.BlockSpec((k, tn), lambda i, j: (0, j)), tab, tab],
        out_specs=pl.BlockSpec((tm, tn), lambda i, j: (i, j)),
        out_shape=jax.ShapeDtypeStruct((m, n), F32),
        compiler_params=_cparams(2), name="qkv_rope",
    )(xn, w_qkv, cos2, sin2)


def _cq_kernel(a_ref, w_ref, g_ref, o_ref):
    o_ref[...] = _rms(_dot(a_ref[...], w_ref[...]), g_ref[...]).astype(o_ref.dtype)


def cq_matmul(xn, w, g, tm=512):
    m, k = xn.shape
    n = w.shape[1]
    return pl.pallas_call(
        _cq_kernel, grid=(m // tm,),
        in_specs=[pl.BlockSpec((tm, k), lambda i: (i, 0)),
                  pl.BlockSpec((k, n), lambda i: (0, 0)),
                  pl.BlockSpec((1, n), lambda i: (0, 0))],
        out_specs=pl.BlockSpec((tm, n), lambda i: (i, 0)),
        out_shape=jax.ShapeDtypeStruct((m, n), BF16),
        compiler_params=_cparams(1), name="mla_cq",
    )(xn, w, g.reshape(1, n))


def _ckv_kernel(a_ref, w_ref, g_ref, cos_ref, sin_ref, ckv_ref, kpe_ref, *, n_kv):
    acc = _dot(a_ref[...], w_ref[...])
    ckv_ref[...] = _rms(acc[:, :n_kv], g_ref[...]).astype(ckv_ref.dtype)
    kpe_ref[...] = _rot_half(acc[:, n_kv:], cos_ref[...], sin_ref[...]).astype(kpe_ref.dtype)


def ckv_matmul(xn, w, g, cos_p, sin_p, seq, tm=512):
    m, k = xn.shape
    n = w.shape[1]
    n_kv = n - LANES
    nsb = seq // tm
    tab = pl.BlockSpec((tm, LANES), lambda i: (i % nsb, 0))
    return pl.pallas_call(
        functools.partial(_ckv_kernel, n_kv=n_kv), grid=(m // tm,),
        in_specs=[pl.BlockSpec((tm, k), lambda i: (i, 0)),
                  pl.BlockSpec((k, n), lambda i: (0, 0)),
                  pl.BlockSpec((1, n_kv), lambda i: (0, 0)), tab, tab],
        out_specs=[pl.BlockSpec((tm, n_kv), lambda i: (i, 0)),
                   pl.BlockSpec((tm, LANES), lambda i: (i, 0))],
        out_shape=[jax.ShapeDtypeStruct((m, n_kv), BF16), jax.ShapeDtypeStruct((m, LANES), BF16)],
        compiler_params=_cparams(1), name="mla_ckv",
    )(xn, w, g.reshape(1, n_kv), cos_p, sin_p)


def _qup_kernel(a_ref, w_ref, cos_ref, sin_ref, o_ref, *, tn, scale):
    acc = _dot(a_ref[...], w_ref[...])
    cos = cos_ref[...]
    sin = sin_ref[...]
    for c in range(tn // MLA_QK):
        lo = c * MLA_QK
        o_ref[:, lo:lo + LANES] = (acc[:, lo:lo + LANES] * scale).astype(o_ref.dtype)
        pe = _rot_half(acc[:, lo + LANES:lo + MLA_QK], cos, sin) * scale
        o_ref[:, lo + LANES:lo + MLA_QK] = pe.astype(o_ref.dtype)


def qup_matmul(cq, w, cos_p, sin_p, seq, tm=1024, tn=512):
    m, k = cq.shape
    n = w.shape[1]
    nsb = seq // tm
    tab = pl.BlockSpec((tm, LANES), lambda i, j: (i % nsb, 0))
    return pl.pallas_call(
        functools.partial(_qup_kernel, tn=tn, scale=(MLA_NOPE + MLA_ROPE) ** -0.5 * np.log2(np.e)),
        grid=(m // tm, n // tn),
        in_specs=[pl.BlockSpec((tm, k), lambda i, j: (i, 0)),
                  pl.BlockSpec((k, tn), lambda i, j: (0, j)), tab, tab],
        out_specs=pl.BlockSpec((tm, tn), lambda i, j: (i, j)),
        out_shape=jax.ShapeDtypeStruct((m, n), BF16),
        compiler_params=_cparams(2), name="mla_qup",
    )(cq, w, cos_p, sin_p)


def _swa_kernel(q0_ref, q1_ref, q2_ref, kc_ref, kp_ref, vc_ref, vp_ref, y_ref, o_sc, lse_sc, *, rows):
    n = pl.program_id(1)
    blk = LOCAL_BLOCK
    qi = lax.broadcasted_iota(jnp.int32, (blk, 2 * blk), 0)
    c = lax.broadcasted_iota(jnp.int32, (blk, 2 * blk), 1)
    band = jnp.where(c <= qi + blk, jnp.where(c >= qi, 0.0, NEG_INF), NEG_INF)
    lo = jnp.maximum(qi, jnp.where(n > 0, 0, blk))
    first = jnp.where(c <= qi + blk, jnp.where(c >= lo, 0.0, NEG_INF), NEG_INF)
    q_refs = (q0_ref, q1_ref, q2_ref)
    for g, (_, d) in enumerate(SWA_PATTERNS):
        span = d * blk
        for s in range(rows // span):
            for r in range(d):
                cur = pl.ds(s * span + r, blk, stride=d)
                q = q_refs[g][cur, :].astype(BF16)
                if s > 0:
                    prev = pl.ds((s - 1) * span + r, blk, stride=d)
                    k_prev, v_prev, bias = kc_ref[prev, :], vc_ref[prev, :], band
                else:
                    prev = pl.ds(rows - span + r, blk, stride=d)
                    k_prev, v_prev, bias = kp_ref[prev, :], vp_ref[prev, :], first
                k2 = jnp.concatenate([k_prev, kc_ref[cur, :]], axis=0).astype(BF16)
                v2 = jnp.concatenate([v_prev, vc_ref[cur, :]], axis=0).astype(BF16)
                sc = _dot_nt(q, k2) + bias
                m = jnp.max(sc, axis=-1, keepdims=True)
                p = jnp.exp(sc - m)
                l = jnp.sum(p, axis=-1, keepdims=True)
                o_sc[g, cur, :] = _dot(p.astype(BF16), v2) / l
                lse_sc[g, cur, :] = jnp.broadcast_to(m + jnp.log(l), (blk, SWA_HD))
    a0, a1, a2 = lse_sc[0], lse_sc[1], lse_sc[2]
    m = jnp.maximum(jnp.maximum(a0, a1), a2)
    e0, e1, e2 = jnp.exp(a0 - m), jnp.exp(a1 - m), jnp.exp(a2 - m)
    y = (e0 * o_sc[0] + e1 * o_sc[1] + e2 * o_sc[2]) / (e0 + e1 + e2)
    y_ref[...] = y.astype(y_ref.dtype)


def swa_attention(qkv, batch, seq, heads):
    w = heads * SWA_HD
    n_groups = len(SWA_PATTERNS)
    rows = max(d for _, d in SWA_PATTERNS) * LOCAL_BLOCK
    assert seq % rows == 0 and all(win // d == LOCAL_BLOCK for win, d in SWA_PATTERNS)
    x = qkv.reshape(batch, seq, (n_groups + 2) * w)

    def spec(col, prev=False):
        if prev:
            return pl.BlockSpec((None, rows, SWA_HD),
                                lambda b, n, h: (b, jnp.maximum(n - 1, 0), col * heads + h))
        return pl.BlockSpec((None, rows, SWA_HD), lambda b, n, h: (b, n, col * heads + h))

    kcol, vcol = n_groups, n_groups + 1
    sc = pltpu.VMEM((n_groups, rows, SWA_HD), F32)
    y = pl.pallas_call(
        functools.partial(_swa_kernel, rows=rows),
        grid=(batch, seq // rows, heads),
        in_specs=[spec(0), spec(1), spec(2), spec(kcol), spec(kcol, True), spec(vcol), spec(vcol, True)],
        out_specs=pl.BlockSpec((None, rows, SWA_HD), lambda b, n, h: (b, n, h)),
        out_shape=jax.ShapeDtypeStruct((batch, seq, w), BF16),
        scratch_shapes=[sc, sc],
        compiler_params=_cparams(3), name="swa_attn",
    )(x, x, x, x, x, x, x)
    return y.reshape(batch * seq, w)


def _conf_kernel(x_ref, halo_ref, w_ref, b_ref, g_ref, beta_ref, y_ref, xs_ref, sh_ref, *, ts, halo):
    i = pl.program_id(1)
    xs_ref[halo:, :] = x_ref[...]
    hal = halo_ref[...]
    xs_ref[:halo, :] = jnp.where(i > 0, hal, jnp.zeros_like(hal))
    acc = jnp.zeros(x_ref.shape, F32) + b_ref[...]
    off = halo - (CONV_K - 1)
    for rho in range(8):
        jmax = (CONV_K - 1 - rho) // 8
        rows = ts + 8 * jmax
        sh_ref[:rows, :] = xs_ref[off + rho:off + rho + rows, :]
        for j in range(jmax + 1):
            k = rho + 8 * j
            acc = acc + sh_ref[8 * j:8 * j + ts, :] * w_ref[k:k + 1, :]
    mu = jnp.mean(acc, axis=-1, keepdims=True)
    d = acc - mu
    var = jnp.mean(d * d, axis=-1, keepdims=True)
    y = d * lax.rsqrt(var + NORM_EPS) * g_ref[...] + beta_ref[...]
    y_ref[...] = _silu(y).astype(y_ref.dtype)


def conformer_conv(glu, w, b, g, beta, batch, seq, ts=256, halo=32):
    c = glu.shape[-1]
    x = glu.reshape(batch, seq, c)
    r = ts // halo
    vec = pl.BlockSpec((1, c), lambda bb, i: (0, 0))
    y = pl.pallas_call(
        functools.partial(_conf_kernel, ts=ts, halo=halo),
        grid=(batch, seq // ts),
        in_specs=[pl.BlockSpec((None, ts, c), lambda bb, i: (bb, i, 0)),
                  pl.BlockSpec((None, halo, c), lambda bb, i: (bb, jnp.maximum(i * r - 1, 0), 0)),
                  pl.BlockSpec((CONV_K, c), lambda bb, i: (0, 0)), vec, vec, vec],
        out_specs=pl.BlockSpec((None, ts, c), lambda bb, i: (bb, i, 0)),
        out_shape=jax.ShapeDtypeStruct((batch, seq, c), BF16),
        scratch_shapes=[pltpu.VMEM((ts + halo, c), F32), pltpu.VMEM((ts + halo, c), F32)],
        compiler_params=_cparams(2), name="conformer_conv",
    )(x, x, w, b.reshape(1, c), g.reshape(1, c), beta.reshape(1, c))
    return y.reshape(batch * seq, c)


def _kvup_kernel(c_ref, wk_ref, wv_ref, kpe_ref, k_ref, v_ref, *, heads):
    c = c_ref[...]
    k = _dot(c, wk_ref[...])
    kpe = kpe_ref[...]
    for h in range(heads):
        k_ref[:, h * MLA_QK:h * MLA_QK + MLA_NOPE] = k[:, h * MLA_NOPE:(h + 1) * MLA_NOPE].astype(k_ref.dtype)
        k_ref[:, h * MLA_QK + MLA_NOPE:(h + 1) * MLA_QK] = kpe
    v_ref[...] = _dot(c, wv_ref[...]).astype(v_ref.dtype)


def kvup_matmul(ckv, w_k, w_v, kpe, heads, tm=512):
    m, k = ckv.shape
    full = lambda n: pl.BlockSpec((k, n), lambda i: (0, 0))
    row = lambda n: pl.BlockSpec((tm, n), lambda i: (i, 0))
    return pl.pallas_call(
        functools.partial(_kvup_kernel, heads=heads), grid=(m // tm,),
        in_specs=[row(k), full(heads * MLA_NOPE), full(heads * MLA_V), row(LANES)],
        out_specs=[row(heads * MLA_QK), row(heads * MLA_V)],
        out_shape=[jax.ShapeDtypeStruct((m, heads * MLA_QK), BF16),
                   jax.ShapeDtypeStruct((m, heads * MLA_V), BF16)],
        compiler_params=_cparams(1), name="mla_kvup",
    )(ckv, w_k, w_v, kpe)


def _mla_kernel(q_ref, k_ref, v_ref, o_ref, m_ref, l_ref, acc_ref, *, tq):
    qi = pl.program_id(2)
    q = q_ref[...]
    m_ref[...] = jnp.full(m_ref.shape, NEG_INF, F32)
    l_ref[...] = jnp.zeros(l_ref.shape, F32)
    acc_ref[...] = jnp.zeros(acc_ref.shape, F32)
    nch = tq // LANES

    def step(kb, masked):
        start = pl.multiple_of(kb * tq, tq)
        s = _dot_nt(q, k_ref[pl.ds(start, tq), :])
        if masked:
            r = lax.broadcasted_iota(jnp.int32, (tq, tq), 0)
            c = lax.broadcasted_iota(jnp.int32, (tq, tq), 1)
            s = jnp.where(c <= r, s, NEG_INF)
        chunks = [s[:, j * LANES:(j + 1) * LANES] for j in range(nch)]
        mx = functools.reduce(jnp.maximum, chunks)
        m_prev = m_ref[...]
        m_new = jnp.maximum(m_prev, jnp.max(mx, axis=-1, keepdims=True))
        alpha = jnp.exp2(m_prev - m_new)
        ps = [jnp.exp2(ch - m_new) for ch in chunks]
        l_ref[...] = alpha * l_ref[...] + functools.reduce(jnp.add, ps)
        p = jnp.concatenate([x.astype(BF16) for x in ps], axis=1)
        acc_ref[...] = alpha * acc_ref[...] + _dot(p, v_ref[pl.ds(start, tq), :])
        m_ref[...] = m_new

    def body(kb, carry):
        step(kb, False)
        return carry

    lax.fori_loop(0, qi, body, 0)
    step(qi, True)
    l = jnp.sum(l_ref[...], axis=-1, keepdims=True)
    o_ref[...] = (acc_ref[...] / l).astype(o_ref.dtype)


def mla_attention(q, k, v, batch, seq, heads, tq=1024):
    q3 = q.reshape(batch, seq, heads * MLA_QK)
    k3 = k.reshape(batch, seq, heads * MLA_QK)
    v3 = v.reshape(batch, seq, heads * MLA_V)
    stat = pltpu.VMEM((tq, LANES), F32)
    y = pl.pallas_call(
        functools.partial(_mla_kernel, tq=tq),
        grid=(batch, heads, seq // tq),
        in_specs=[pl.BlockSpec((None, tq, MLA_QK), lambda b, h, i: (b, i, h)),
                  pl.BlockSpec((None, seq, MLA_QK), lambda b, h, i: (b, 0, h)),
                  pl.BlockSpec((None, seq, MLA_V), lambda b, h, i: (b, 0, h))],
        out_specs=pl.BlockSpec((None, tq, MLA_V), lambda b, h, i: (b, i, h)),
        out_shape=jax.ShapeDtypeStruct((batch, seq, heads * MLA_V), BF16),
        scratch_shapes=[stat, stat, pltpu.VMEM((tq, MLA_V), F32)],
        compiler_params=_cparams(3), name="mla_attn",
    )(q3, k3, v3)
    return y.reshape(batch * seq, heads * MLA_V)


def _lru_gates_kernel(x_ref, halo_ref, wc_ref, bc_ref, wa_ref, ba_ref, wx_ref, bx_ref, lam_ref,
                      a_ref, b_ref, xs_ref, *, ts, halo, bw):
    i = pl.program_id(1)
    xs_ref[halo:, :] = x_ref[...]
    hal = halo_ref[...]
    xs_ref[:halo, :] = jnp.where(i > 0, hal, jnp.zeros_like(hal))
    u = jnp.zeros(x_ref.shape, F32) + bc_ref[...]
    off = halo - (LRU_CONV_K - 1)
    for k in range(LRU_CONV_K):
        u = u + xs_ref[off + k:off + k + ts, :] * wc_ref[k:k + 1, :]
    lam = lam_ref[...]
    neg = -lam
    softplus = jnp.maximum(neg, 0.0) + jnp.log1p(jnp.exp(-jnp.abs(neg)))
    for n in range(LRU_BLOCKS):
        sl = slice(n * bw, (n + 1) * bw)
        ub = u[:, sl]
        ubf = ub.astype(BF16)
        r = _sigmoid(_dot(ubf, wa_ref[n]) + ba_ref[:, sl])
        g = _sigmoid(_dot(ubf, wx_ref[n]) + bx_ref[:, sl])
        log_a = (-LRU_C) * r * softplus[:, sl]
        a_ref[:, sl] = jnp.exp(log_a)
        th = jnp.tanh(log_a)
        b_ref[:, sl] = jnp.sqrt(-2.0 * th / (1.0 - th)) * (g * ub)


def lru_gates(lru_in, wc, bc, wa, ba, wx, bx, lam, batch, seq, ts=256, halo=8):
    c = lru_in.shape[-1]
    bw = c // LRU_BLOCKS
    x = lru_in.reshape(batch, seq, c)
    r = ts // halo
    vec = pl.BlockSpec((1, c), lambda bb, i: (0, 0))
    blk = pl.BlockSpec((None, ts, c), lambda bb, i: (bb, i, 0))
    wspec = pl.BlockSpec((LRU_BLOCKS, bw, bw), lambda bb, i: (0, 0, 0))
    sds = jax.ShapeDtypeStruct((batch, seq, c), F32)
    return pl.pallas_call(
        functools.partial(_lru_gates_kernel, ts=ts, halo=halo, bw=bw),
        grid=(batch, seq // ts),
        in_specs=[blk,
                  pl.BlockSpec((None, halo, c), lambda bb, i: (bb, jnp.maximum(i * r - 1, 0), 0)),
                  pl.BlockSpec((LRU_CONV_K, c), lambda bb, i: (0, 0)), vec,
                  wspec, vec, wspec, vec, vec],
        out_specs=[blk, blk], out_shape=[sds, sds],
        scratch_shapes=[pltpu.VMEM((ts + halo, c), F32)],
        compiler_params=_cparams(2), name="lru_gates",
    )(x, x, wc, bc.reshape(1, c), wa, ba.reshape(1, c), wx, bx.reshape(1, c), lam.reshape(1, c))


def _lru_scan_kernel(a_ref, b_ref, gate_ref, y_ref, h_ref, hs_ref, *, ts):
    @pl.when(pl.program_id(1) == 0)
    def _():
        h_ref[...] = jnp.zeros(h_ref.shape, F32)

    def body(t, h):
        h = a_ref[pl.ds(t, 1), :] * h + b_ref[pl.ds(t, 1), :]
        hs_ref[pl.ds(t, 1), :] = h
        return h

    h_ref[...] = lax.fori_loop(0, ts, body, h_ref[...], unroll=8)
    y_ref[...] = (hs_ref[...] * gate_ref[...]).astype(y_ref.dtype)


def lru_scan(a, b, gate, batch, seq, ts=256):
    c = a.shape[-1]
    blk = pl.BlockSpec((None, ts, c), lambda bb, i: (bb, i, 0))
    y = pl.pallas_call(
        functools.partial(_lru_scan_kernel, ts=ts),
        grid=(batch, seq // ts),
        in_specs=[blk, blk, blk], out_specs=blk,
        out_shape=jax.ShapeDtypeStruct((batch, seq, c), BF16),
        scratch_shapes=[pltpu.VMEM((1, c), F32), pltpu.VMEM((ts, c), F32)],
        compiler_params=_cparams(2), name="lru_scan",
    )(a, b, gate.reshape(batch, seq, c))
    return y.reshape(batch * seq, c)


def _merge_kernel(xn_ref, wg_ref, bg_ref, y_ref, wb_ref, o_ref, acc_ref):
    i = pl.program_id(2)
    gate = _sigmoid(_dot(xn_ref[...], wg_ref[...].astype(BF16)) + bg_ref[...])
    contrib = gate * _dot(y_ref[...], wb_ref[...])

    @pl.when(i == 0)
    def _():
        acc_ref[...] = contrib

    @pl.when(i > 0)
    def _():
        acc_ref[...] += contrib

    @pl.when(i == pl.num_programs(2) - 1)
    def _():
        o_ref[...] = acc_ref[...].astype(o_ref.dtype)


def gated_merge(xn, w_gate, b_gate, ys, w_bo, layer, tm=1024, tn=512):
    m, d = xn.shape
    nlay, nbr, bwid, _ = w_bo.shape
    return pl.pallas_call(
        _merge_kernel,
        grid=(m // tm, d // tn, nbr),
        in_specs=[pl.BlockSpec((tm, d), lambda a, j, i: (a, 0), pipeline_mode=pl.Buffered(1)),
                  pl.BlockSpec((None, None, d, tn), lambda a, j, i: (layer, i, 0, j)),
                  pl.BlockSpec((None, None, 1, tn), lambda a, j, i: (layer, i, 0, j)),
                  pl.BlockSpec((None, tm, bwid), lambda a, j, i: (i, a, 0)),
                  pl.BlockSpec((None, None, bwid, tn), lambda a, j, i: (layer, i, 0, j))],
        out_specs=pl.BlockSpec((tm, tn), lambda a, j, i: (a, j)),
        out_shape=jax.ShapeDtypeStruct((m, d), BF16),
        scratch_shapes=[pltpu.VMEM((tm, tn), F32)],
        compiler_params=_cparams(3), name="gated_merge",
    )(xn, w_gate, b_gate.reshape(nlay, nbr, 1, d), ys, w_bo)


def _xattn_kernel(a_ref, w_ref, k_ref, v_ref, o_ref, *, scale):
    q = (_dot(a_ref[...], w_ref[...]) * scale).astype(BF16)
    s = _dot_nt(q, k_ref[...])
    m = jnp.max(s, axis=-1, keepdims=True)
    p = jnp.exp(s - m)
    l = jnp.sum(p, axis=-1, keepdims=True)
    o_ref[...] = (_dot(p.astype(BF16), v_ref[...]) / l).astype(o_ref.dtype)


def cross_attention(hn, w_xq, kx, vx, seq, mem_len, layer=None, tm=1024):
    m, d = hn.shape
    spb = seq // tm
    kvspec = pl.BlockSpec((mem_len, X_HD), lambda i, h: (i // spb, h))
    return pl.pallas_call(
        functools.partial(_xattn_kernel, scale=X_HD ** -0.5),
        grid=(m // tm, X_HEADS),
        in_specs=[pl.BlockSpec((tm, d), lambda i, h: (i, 0)),
                  _w_spec(w_xq, layer, X_HD, lambda i, h: h), kvspec, kvspec],
        out_specs=pl.BlockSpec((tm, X_HD), lambda i, h: (i, h)),
        out_shape=jax.ShapeDtypeStruct((m, X_HEADS * X_HD), BF16),
        compiler_params=_cparams(2), name="cross_attn",
    )(hn, w_xq, kx, vx)


def _rope_tables(seq):
    def tables(dim):
        inv_freq = ROPE_THETA ** (-jnp.arange(0, dim, 2, dtype=F32) / dim)
        ang = jnp.arange(seq, dtype=F32)[:, None] * inv_freq[None, :]
        return jnp.cos(ang), jnp.sin(ang)

    cos_a, sin_a = tables(SWA_HD)
    cos2 = jnp.concatenate([cos_a, cos_a], axis=-1)
    sin2 = jnp.concatenate([-sin_a, sin_a], axis=-1)
    cos_c, sin_c = tables(MLA_ROPE)
    z = jnp.zeros_like(cos_c)
    cos_p = jnp.concatenate([cos_c, z, cos_c, z], axis=-1)
    sin_p = jnp.concatenate([-sin_c, z, sin_c, z], axis=-1)
    return cos2, sin2, cos_p, sin_p


def _pe_layout(w):
    half = MLA_ROPE // 2
    z = jnp.zeros(w.shape[:-1] + (half,), w.dtype)
    return jnp.concatenate([w[..., :half], z, w[..., half:], z], axis=-1)


def kernel(x, mem, g_mix_pre, w_in, w_mla_q_up, g_mla_q, w_mla_kv_up, g_mla_kv, w_conf_dw, b_conf_dw, g_conf_ln, b_conf_ln, w_lru_conv, b_lru_conv, w_lru_a, b_lru_a, w_lru_x, b_lru_x, lru_lambda, w_branch_out, w_gate, b_gate, w_o, g_mix_post, g_x_pre, g_mem, w_xq, w_xk, w_xv, w_xo, g_x_post, g_ffn_pre, w_ffn_gate, w_ffn_up, w_ffn_down, g_ffn_post):
    batch, seq, d = x.shape
    depth = w_in.shape[0]
    mem_len = mem.shape[1]
    bw = d // 4
    heads = bw // SWA_HD
    n_groups = len(SWA_PATTERNS)
    q_lora = w_mla_q_up.shape[1]
    kv_lora = w_mla_kv_up.shape[1]
    t = batch * seq

    cuts = np.cumsum([n_groups * bw, bw, bw, bw, bw, q_lora, kv_lora, MLA_ROPE, bw, bw]).tolist()
    c_q, c_k, c_v, c_cv, c_cg, c_cq, c_ckv, c_kpe, c_li, c_lg = cuts
    cos2, sin2, cos_p, sin_p = _rope_tables(seq)

    wbo_b, wo_b = w_branch_out.astype(BF16), w_o.astype(BF16)
    wxq_b, wxo_b = w_xq.astype(BF16), w_xo.astype(BF16)
    wfd_b = w_ffn_down.astype(BF16)

    h = x.reshape(t, d)
    mem2 = mem.reshape(batch * mem_len, d)
    hn = rms_cast(h, g_mix_pre[0])

    for l in range(depth):
        wl = w_in[l]
        w_qkv = wl[:, :c_v].astype(BF16)
        w_cval = wl[:, c_v:c_cv].astype(BF16)
        w_cgate = wl[:, c_cv:c_cg].astype(BF16)
        w_cq = wl[:, c_cg:c_cq].astype(BF16)
        w_ckv = jnp.concatenate([wl[:, c_cq:c_ckv], _pe_layout(wl[:, c_ckv:c_kpe])], axis=-1).astype(BF16)
        w_lin = wl[:, c_kpe:c_li].astype(BF16)
        w_lgate = wl[:, c_li:c_lg].astype(BF16)
        wq = w_mla_q_up[l].reshape(q_lora, heads, MLA_NOPE + MLA_ROPE)
        w_qup = jnp.concatenate([wq[..., :MLA_NOPE], _pe_layout(wq[..., MLA_NOPE:])], axis=-1)
        w_qup = w_qup.reshape(q_lora, heads * MLA_QK).astype(BF16)

        qkv = qkv_matmul(hn, w_qkv, cos2, sin2, seq, n_groups * bw, bw)
        y_a = swa_attention(qkv, batch, seq, heads)

        glu = dual_matmul(hn, w_cval, w_cgate, "glu", F32, 1024, 512, "conf_glu")
        y_b = conformer_conv(glu, w_conf_dw[l], b_conf_dw[l], g_conf_ln[l], b_conf_ln[l], batch, seq)

        cq = cq_matmul(hn, w_cq, g_mla_q[l])
        ckv, kpe = ckv_matmul(hn, w_ckv, g_mla_kv[l], cos_p, sin_p, seq)
        q_c = qup_matmul(cq, w_qup, cos_p, sin_p, seq)
        wkv = w_mla_kv_up[l].reshape(kv_lora, heads, MLA_NOPE + MLA_V)
        w_kup = wkv[..., :MLA_NOPE].reshape(kv_lora, heads * MLA_NOPE).astype(BF16)
        w_vup = wkv[..., MLA_NOPE:].reshape(kv_lora, heads * MLA_V).astype(BF16)
        k_c, v_c = kvup_matmul(ckv, w_kup, w_vup, kpe, heads)
        y_c = mla_attention(q_c, k_c, v_c, batch, seq, heads)

        lru_in = matmul(hn, w_lin, F32, 1024, 512, name="lru_in")
        gate = matmul(hn, w_lgate, F32, 1024, 512, act="gelu", name="lru_gate")
        a, b = lru_gates(lru_in, w_lru_conv[l], b_lru_conv[l], w_lru_a[l].astype(BF16), b_lru_a[l],
                         w_lru_x[l].astype(BF16), b_lru_x[l], lru_lambda[l], batch, seq)
        y_d = lru_scan(a, b, gate, batch, seq)

        ys = jnp.stack([y_a, y_b, y_c, y_d])
        merged = gated_merge(hn, w_gate, b_gate, ys, wbo_b, l)
        mix = matmul(merged, wo_b, F32, 1024, 512, layer=l, name="w_o")
        h, hn = resid_norm(h, mix, g_mix_post[l], g_x_pre[l])

        mem_n = rms_cast(mem2, g_mem[l])
        tmem = min(512, batch * mem_len)
        kx = matmul(mem_n, w_xk[l].astype(BF16), BF16, tmem, 512, name="x_k")
        vx = matmul(mem_n, w_xv[l].astype(BF16), BF16, tmem, 512, name="x_v")
        xo = cross_attention(hn, wxq_b, kx, vx, seq, mem_len, layer=l)
        h, hn = matmul_resid_norm(xo, wxo_b, h, g_x_post[l], g_ffn_pre[l], layer=l)

        act = dual_matmul(hn, w_ffn_gate, w_ffn_up, "swiglu", BF16, 1024, 256, "ffn_up", layer=l)
        ff = matmul(act, wfd_b, F32, 512, 256, layer=l, name="ffn_down")
        g_next = g_mix_pre[l + 1] if l + 1 < depth else None
        h, hn = resid_norm(h, ff, g_ffn_post[l], g_next)

    return h.reshape(batch, seq, d)
```
